```python
import math
import jax, jax.numpy as jnp
from jax import lax
import numpy as np

D_MODEL = 1024
BATCH = 16
SEQ = 4096
DEPTH = 1

D_MIX = D_MODEL
MLSTM_WIDTH = D_MIX // 2
MLSTM_HEADS = 4
MLSTM_HEAD_DIM = MLSTM_WIDTH // MLSTM_HEADS
MLSTM_CHUNK = 64
QK_CONV_WIDTH = 4
CONV_WIDTH_CH = D_MIX - MLSTM_WIDTH
CONV_HEADS = 4
CONV_KERNEL = 31
D_FF = int(math.ceil((8 * D_MODEL / 3) / 256) * 256)
NORM_EPS = 1e-6
F_BIAS_INIT = 3.0

IN_SPLITS = [
    MLSTM_WIDTH,
    MLSTM_WIDTH,
    MLSTM_WIDTH,
    MLSTM_WIDTH,
    MLSTM_HEADS,
    MLSTM_HEADS,
    CONV_WIDTH_CH,
    CONV_WIDTH_CH,
]
D_IN = sum(IN_SPLITS)

kernel_name = "hymba_mlstm_conformer_conv_sandwich"


def rms_norm(x, w):
    xf = x.astype(jnp.float32)
    y = xf * lax.rsqrt(jnp.mean(xf * xf, axis=-1, keepdims=True) + NORM_EPS)
    return (y * w.astype(jnp.float32)).astype(x.dtype)


def group_layer_norm(x, groups, w, b=None):
    shp = x.shape
    xf = x.astype(jnp.float32).reshape(shp[:-1] + (groups, shp[-1] // groups))
    mu = jnp.mean(xf, axis=-1, keepdims=True)
    var = jnp.mean(jnp.square(xf - mu), axis=-1, keepdims=True)
    y = ((xf - mu) * lax.rsqrt(var + NORM_EPS)).reshape(shp) * w.astype(jnp.float32)
    if b is not None:
        y = y + b.astype(jnp.float32)
    return y.astype(x.dtype)


def causal_depthwise_conv(x, w, b):
    K, C = w.shape
    y = lax.conv_general_dilated(
        x, w[:, None, :].astype(x.dtype), window_strides=(1,), padding=((K - 1, 0),),
        dimension_numbers=("NWC", "WIO", "NWC"), feature_group_count=C)
    return y + b.astype(x.dtype)


def mlstm_chunkwise(q, k, v, i_pre, f_pre):
    B, H, S, D = q.shape
    L = MLSTM_CHUNK
    NC = S // L
    f32 = jnp.float32
    q = q.astype(f32) * (D ** -0.5)
    k = k.astype(f32)
    v = v.astype(f32)
    log_i = i_pre.astype(f32)
    log_f = jax.nn.log_sigmoid(f_pre.astype(f32))

    def chunked(t):
        return jnp.moveaxis(t.reshape((B, H, NC, L) + t.shape[3:]), 2, 0)

    causal = jnp.tril(jnp.ones((L, L), dtype=bool))

    def step(carry, inp):
        C, n, m = carry
        qc, kc, vc, ic, fc = inp
        b = jnp.cumsum(fc, axis=-1)
        dmat = jnp.where(causal, b[..., :, None] - b[..., None, :] + ic[..., None, :], -jnp.inf)
        inter = b + m[..., None]
        m_t = jnp.maximum(inter, jnp.max(dmat, axis=-1))
        w_inter = jnp.exp(inter - m_t)
        s = jnp.einsum("bhtd,bhsd->bhts", qc, kc) * jnp.exp(dmat - m_t[..., None])
        num = w_inter[..., None] * jnp.einsum("bhtd,bhde->bhte", qc, C) \
            + jnp.einsum("bhts,bhse->bhte", s, vc)
        den = w_inter * jnp.einsum("bhtd,bhd->bht", qc, n) + jnp.sum(s, axis=-1)
        h = num / jnp.maximum(jnp.abs(den), jnp.exp(-m_t))[..., None]
        b_last = b[..., -1]
        a = b_last[..., None] - b + ic
        m_new = jnp.maximum(b_last + m, jnp.max(a, axis=-1))
        decay = jnp.exp(b_last + m - m_new)
        wa = jnp.exp(a - m_new[..., None])
        C_new = decay[..., None, None] * C + jnp.einsum("bhs,bhsd,bhse->bhde", wa, kc, vc)
        n_new = decay[..., None] * n + jnp.einsum("bhs,bhsd->bhd", wa, kc)
        return (C_new, n_new, m_new), h

    init = (jnp.zeros((B, H, D, D), f32), jnp.zeros((B, H, D), f32), jnp.zeros((B, H), f32))
    _, hs = lax.scan(step, init, (chunked(q), chunked(k), chunked(v), chunked(log_i), chunked(log_f)))
    return jnp.moveaxis(hs, 0, 2).reshape(B, H, S, D)


def hybrid_mixer(h, w_in, qk_conv_w, qk_conv_b, i_bias, f_bias, mh_norm_w,
                 dw_conv_w, dw_conv_b, conv_norm_w, conv_norm_b, w_out):
    B, S, _ = h.shape
    proj = jnp.einsum("bsd,df->bsf", h, w_in.astype(h.dtype))
    offs = np.cumsum(IN_SPLITS)[:-1].tolist()
    q, k, v, o, ig, fg, cv, cg = jnp.split(proj, offs, axis=-1)

    qk = jax.nn.silu(causal_depthwise_conv(jnp.concatenate([q, k], axis=-1), qk_conv_w, qk_conv_b))
    q, k = jnp.split(qk, 2, axis=-1)

    def heads(t):
        return t.reshape(B, S, MLSTM_HEADS, MLSTM_HEAD_DIM).transpose(0, 2, 1, 3)

    i_pre = (ig + i_bias.astype(ig.dtype)).transpose(0, 2, 1)
    f_pre = (fg + f_bias.astype(fg.dtype)).transpose(0, 2, 1)
    cell = mlstm_chunkwise(heads(q), heads(k), heads(v), i_pre, f_pre)
    cell = cell.transpose(0, 2, 1, 3).reshape(B, S, MLSTM_WIDTH).astype(h.dtype)
    y_mlstm = jax.nn.sigmoid(o) * group_layer_norm(cell, MLSTM_HEADS, mh_norm_w)

    u = cv * jax.nn.sigmoid(cg)
    u = causal_depthwise_conv(u, dw_conv_w, dw_conv_b)
    y_conv = jax.nn.silu(group_layer_norm(u, CONV_HEADS, conv_norm_w, conv_norm_b))

    y = jnp.concatenate([y_mlstm, y_conv], axis=-1)
    return jnp.einsum("bsf,fd->bsd", y, w_out.astype(h.dtype))


def swiglu_ffn(h, w_gate, w_up, w_down):
    g = jnp.einsum("bsd,df->bsf", h, w_gate.astype(h.dtype))
    u = jnp.einsum("bsd,df->bsf", h, w_up.astype(h.dtype))
    return jnp.einsum("bsf,fd->bsd", jax.nn.silu(g) * u, w_down.astype(h.dtype))


def setup_inputs(seed: int = 0) -> dict:
    key = jax.random.key(seed)
    ks = jax.random.split(key, 20)
    f32 = jnp.float32

    def nrm(k, shape, scale):
        return jax.random.normal(k, shape, f32) * scale

    def gain(k, shape):
        return 1.0 + 0.05 * jax.random.normal(k, shape, f32)

    L = DEPTH
    return {
        "x": jax.random.normal(ks[0], (BATCH, SEQ, D_MODEL), f32),
        "ln_mix_pre": gain(ks[1], (L, D_MODEL)),
        "ln_mix_post": gain(ks[2], (L, D_MODEL)),
        "w_in": nrm(ks[3], (L, D_MODEL, D_IN), D_MODEL ** -0.5),
        "qk_conv_w": nrm(ks[4], (L, QK_CONV_WIDTH, 2 * MLSTM_WIDTH), QK_CONV_WIDTH ** -0.5),
        "qk_conv_b": nrm(ks[5], (L, 2 * MLSTM_WIDTH), 0.02),
        "i_bias": nrm(ks[6], (L, MLSTM_HEADS), 0.1),
        "f_bias": F_BIAS_INIT + nrm(ks[7], (L, MLSTM_HEADS), 0.1),
        "mh_norm_w": gain(ks[8], (L, MLSTM_WIDTH)),
        "dw_conv_w": nrm(ks[9], (L, CONV_KERNEL, CONV_WIDTH_CH), CONV_KERNEL ** -0.5),
        "dw_conv_b": nrm(ks[10], (L, CONV_WIDTH_CH), 0.02),
        "conv_norm_w": gain(ks[11], (L, CONV_WIDTH_CH)),
        "conv_norm_b": nrm(ks[12], (L, CONV_WIDTH_CH), 0.02),
        "w_out": nrm(ks[13], (L, D_MIX, D_MODEL), D_MIX ** -0.5),
        "ln_ffn_pre": gain(ks[14], (L, D_MODEL)),
        "ln_ffn_post": gain(ks[15], (L, D_MODEL)),
        "w_gate": nrm(ks[16], (L, D_MODEL, D_FF), D_MODEL ** -0.5),
        "w_up": nrm(ks[17], (L, D_MODEL, D_FF), D_MODEL ** -0.5),
        "w_down": nrm(ks[18], (L, D_FF, D_MODEL), D_FF ** -0.5),
    }


def reference(x, ln_mix_pre, ln_mix_post, w_in, qk_conv_w, qk_conv_b, i_bias, f_bias,
              mh_norm_w, dw_conv_w, dw_conv_b, conv_norm_w, conv_norm_b, w_out,
              ln_ffn_pre, ln_ffn_post, w_gate, w_up, w_down):
    h = x
    for l in range(DEPTH):
        mix = hybrid_mixer(rms_norm(h, ln_mix_pre[l]), w_in[l], qk_conv_w[l], qk_conv_b[l],
                           i_bias[l], f_bias[l], mh_norm_w[l], dw_conv_w[l], dw_conv_b[l],
                           conv_norm_w[l], conv_norm_b[l], w_out[l])
        h = h + rms_norm(mix, ln_mix_post[l])
        ff = swiglu_ffn(rms_norm(h, ln_ffn_pre[l]), w_gate[l], w_up[l], w_down[l])
        h = h + rms_norm(ff, ln_ffn_post[l])
    return h
```

```python
import functools

import jax
import jax.numpy as jnp
from jax import lax
from jax.experimental import pallas as pl
from jax.experimental.pallas import tpu as pltpu

F32 = jnp.float32
BF16 = jnp.bfloat16

NORM_EPS = 1e-6
LANES = 128
SUBLANES = 8
MXU_N = 256
VMEM_BYTES = 64 * 1024 * 1024
CHUNK = 128


def _rms(x, w):
    return x * lax.rsqrt(jnp.mean(x * x, axis=-1, keepdims=True) + NORM_EPS) * w


def _sigmoid(x):
    return 1.0 / (1.0 + jnp.exp(-x))


def _log_sigmoid(x):
    return -(jnp.maximum(-x, 0.0) + jnp.log1p(jnp.exp(-jnp.abs(x))))


def _group_norm_lanes(x, w, b=None):
    mu = jnp.mean(x, axis=-1, keepdims=True)
    xc = x - mu
    var = jnp.mean(xc * xc, axis=-1, keepdims=True)
    y = xc * lax.rsqrt(var + NORM_EPS) * w
    return y if b is None else y + b


def _in_proj_kernel(x_ref, g_ref, wm_ref, wg_ref, main_ref, gates_ref, *, n_chunk):
    xn = _rms(x_ref[...], g_ref[...]).astype(BF16)
    n_main = wm_ref.shape[1]
    for c in range(n_main // n_chunk):
        sl = slice(c * n_chunk, (c + 1) * n_chunk)
        main_ref[:, sl] = jnp.dot(xn, wm_ref[:, sl], preferred_element_type=F32).astype(BF16)
    gates_ref[...] = jnp.dot(xn, wg_ref[...], preferred_element_type=F32)


def _in_proj(x2d, g, w_main, w_gate, *, tm):
    t, d = x2d.shape
    n_main = w_main.shape[1]
    const = lambda i: (0, 0)
    vmem = (2 * tm * d * 4 + 2 * tm * n_main * 2 + 2 * tm * LANES * 4
            + 2 * (d * n_main + d * LANES) * 2 + 6 * tm * d * 4)
    return pl.pallas_call(
        functools.partial(_in_proj_kernel, n_chunk=512),
        out_shape=(jax.ShapeDtypeStruct((t, n_main), BF16), jax.ShapeDtypeStruct((t, LANES), F32)),
        grid=(t // tm,),
        in_specs=[
            pl.BlockSpec((tm, d), lambda i: (i, 0)),
            pl.BlockSpec((1, d), const),
            pl.BlockSpec((d, n_main), const),
            pl.BlockSpec((d, LANES), const),
        ],
        out_specs=(pl.BlockSpec((tm, n_main), lambda i: (i, 0)), pl.BlockSpec((tm, LANES), lambda i: (i, 0))),
        compiler_params=pltpu.CompilerParams(
            dimension_semantics=("arbitrary",), vmem_limit_bytes=min(vmem, VMEM_BYTES - (8 << 20))),
        name="in_proj",
    )(x2d, g, w_main, w_gate)


def _mixer_kernel(qk_ref, v_ref, o_ref, cvcg_ref, gates_ref,
                  qkw_ref, qkb_ref, gbias_ref, mhw_ref, dww_ref, dwb_ref, cnw_ref, cnb_ref,
                  y_ref,
                  qkext_ref, uext_ref, q_s, k_s, c_ref, m_ref,
                  *, n_heads, qk_taps, dw_taps, row_blk):
    ts = qk_ref.shape[0]
    mw = v_ref.shape[1]
    cw = uext_ref.shape[1]
    hd = mw // n_heads
    L = CHUNK
    qk_pad = qkext_ref.shape[0] - ts
    u_pad = uext_ref.shape[0] - ts

    @pl.when(pl.program_id(1) == 0)
    def _():
        qkext_ref[0:qk_pad, :] = jnp.zeros((qk_pad, 2 * mw), F32)
        uext_ref[0:u_pad, :] = jnp.zeros((u_pad, cw), F32)
        c_ref[...] = jnp.zeros(c_ref.shape, F32)
        m_ref[...] = jnp.zeros(m_ref.shape, F32)

    q_scale = hd ** -0.5
    for blk in range(ts // row_blk):
        r0 = blk * row_blk
        qkext_ref[qk_pad + r0:qk_pad + r0 + row_blk, :] = qk_ref[r0:r0 + row_blk, :].astype(F32)
        acc = jnp.broadcast_to(qkb_ref[...], (row_blk, 2 * mw))
        for k in range(qk_taps):
            off = qk_pad - (qk_taps - 1) + k + r0
            acc = acc + qkw_ref[k:k + 1, :] * qkext_ref[off:off + row_blk, :]
        act = acc * _sigmoid(acc)
        q_s[r0:r0 + row_blk, :] = (act[:, :mw] * q_scale).astype(BF16)
        k_s[r0:r0 + row_blk, :] = act[:, mw:].astype(BF16)
    qkext_ref[0:qk_pad, :] = qkext_ref[ts:ts + qk_pad, :]

    for blk in range(ts // row_blk):
        r0 = blk * row_blk
        cv = cvcg_ref[r0:r0 + row_blk, 0:cw].astype(F32)
        cg = cvcg_ref[r0:r0 + row_blk, cw:2 * cw].astype(F32)
        uext_ref[u_pad + r0:u_pad + r0 + row_blk, :] = cv * _sigmoid(cg)
        acc = jnp.broadcast_to(dwb_ref[...], (row_blk, cw))
        for k in range(dw_taps):
            off = u_pad - (dw_taps - 1) + k + r0
            acc = acc + dww_ref[k:k + 1, :] * uext_ref[off:off + row_blk, :]
        for g in range(cw // LANES):
            gl = slice(g * LANES, (g + 1) * LANES)
            yn = _group_norm_lanes(acc[:, gl], cnw_ref[:, gl], cnb_ref[:, gl])
            y_ref[r0:r0 + row_blk, mw + g * LANES:mw + (g + 1) * LANES] = (yn * _sigmoid(yn)).astype(BF16)
    uext_ref[0:u_pad, :] = uext_ref[ts:ts + u_pad, :]

    row_i = lax.broadcasted_iota(jnp.int32, (L, L), 0)
    col_i = lax.broadcasted_iota(jnp.int32, (L, L), 1)
    causal = row_i >= col_i
    tril = jnp.where(causal, 1.0, 0.0).astype(BF16)
    lane_i = lax.broadcasted_iota(jnp.int32, (L, LANES), 1)
    ones_col = jnp.where(lane_i == 0, 1.0, 0.0).astype(BF16)

    for c in range(ts // L):
        rows = slice(c * L, (c + 1) * L)
        g_pre = gates_ref[rows, :] + gbias_ref[...]
        logf = _log_sigmoid(g_pre)
        hi = logf.astype(BF16)
        r1 = logf - hi.astype(F32)
        mid = r1.astype(BF16)
        lo = (r1 - mid.astype(F32)).astype(BF16)
        b_all = (jnp.dot(tril, lo, preferred_element_type=F32) + jnp.dot(tril, mid, preferred_element_type=F32)
                 + jnp.dot(tril, hi, preferred_element_type=F32))
        gc = jnp.where(lane_i < n_heads, g_pre, b_all)
        gct = gc.T

        for h in range(n_heads):
            hl = slice(h * hd, (h + 1) * hd)
            i_row = gct[h:h + 1, :]
            b_row = gct[n_heads + h:n_heads + h + 1, :]
            bb = jnp.broadcast_to(gc[:, n_heads + h:n_heads + h + 1], (L, LANES))
            ib = jnp.broadcast_to(gc[:, h:h + 1], (L, LANES))
            m_prev = m_ref[h]

            dm = jnp.where(causal, bb + (i_row - b_row), -jnp.inf)
            inter = bb + m_prev
            m_t = jnp.maximum(inter, jnp.max(dm, axis=-1, keepdims=True))
            w_inter = jnp.exp(inter - m_t)
            p = jnp.exp(dm - m_t)

            qc = q_s[rows, hl]
            kc = k_s[rows, hl]
            v_aug = jnp.concatenate([v_ref[rows, hl], ones_col], axis=1)
            s = lax.dot_general(qc, kc, (((1,), (1,)), ((), ())), preferred_element_type=F32) * p
            c_aug = c_ref[h]
            r = (jnp.concatenate([w_inter, w_inter], axis=1)
                 * jnp.dot(qc, c_aug.astype(BF16), preferred_element_type=F32)
                 + jnp.dot(s.astype(BF16), v_aug, preferred_element_type=F32))
            den = r[:, hd:hd + 1]
            cell = r[:, :hd] * (1.0 / jnp.maximum(jnp.abs(den), jnp.exp(-m_t[:, 0:1])))

            b_last = bb[L - 1:L, :]
            a_col = ib - bb
            m_new = b_last + jnp.maximum(m_prev, jnp.max(a_col, axis=0, keepdims=True))
            decay = jnp.exp(b_last + m_prev - m_new)
            kw = kc.astype(F32) * jnp.exp(a_col + b_last - m_new)
            c_ref[h] = (jnp.concatenate([decay, decay], axis=1) * c_aug
                        + jnp.dot(kw.T.astype(BF16), v_aug, preferred_element_type=F32))
            m_ref[h] = m_new

            yn = _group_norm_lanes(cell, mhw_ref[:, hl])
            y_ref[rows, hl] = (_sigmoid(o_ref[rows, hl].astype(F32)) * yn).astype(BF16)


def _mixer(main, gates, qkw, qkb, gbias, mhw, dww, dwb, cnw, cnb, *, n_heads, ts):
    b, s, n_main = main.shape
    mw = mhw.shape[1]
    cw = dwb.shape[1]
    hd = mw // n_heads
    assert hd == LANES and CHUNK == hd and n_main == 4 * mw + 2 * cw and 2 * cw == 2 * mw
    qk_taps, dw_taps = qkw.shape[0], dww.shape[0]
    qk_pad = -(-(qk_taps - 1) // SUBLANES) * SUBLANES
    u_pad = -(-(dw_taps - 1) // SUBLANES) * SUBLANES
    const = lambda i, j: (0, 0)
    kern = functools.partial(_mixer_kernel, n_heads=n_heads, qk_taps=qk_taps, dw_taps=dw_taps, row_blk=64)
    vmem = (2 * ts * (2 * mw + mw + mw + 2 * cw) * 2 + 2 * ts * LANES * 4 + 2 * ts * (mw + cw) * 2
            + (ts + qk_pad) * 2 * mw * 4 + (ts + u_pad) * cw * 4 + 2 * ts * mw * 2
            + n_heads * hd * 2 * hd * 4 + (16 << 20))
    return pl.pallas_call(
        kern,
        out_shape=jax.ShapeDtypeStruct((b, s, mw + cw), BF16),
        grid=(b, s // ts),
        in_specs=[
            pl.BlockSpec((None, ts, 2 * mw), lambda i, j: (i, j, 0)),
            pl.BlockSpec((None, ts, mw), lambda i, j: (i, j, 2)),
            pl.BlockSpec((None, ts, mw), lambda i, j: (i, j, 3)),
            pl.BlockSpec((None, ts, 2 * cw), lambda i, j: (i, j, 2)),
            pl.BlockSpec((None, ts, LANES), lambda i, j: (i, j, 0)),
            pl.BlockSpec((qk_taps, 2 * mw), const),
            pl.BlockSpec((1, 2 * mw), const),
            pl.BlockSpec((1, LANES), const),
            pl.BlockSpec((1, mw), const),
            pl.BlockSpec((dw_taps, cw), const),
            pl.BlockSpec((1, cw), const),
            pl.BlockSpec((1, cw), const),
            pl.BlockSpec((1, cw), const),
        ],
        out_specs=pl.BlockSpec((None, ts, mw + cw), lambda i, j: (i, j, 0)),
        scratch_shapes=[
            pltpu.VMEM((ts + qk_pad, 2 * mw), F32),
            pltpu.VMEM((ts + u_pad, cw), F32),
            pltpu.VMEM((ts, mw), BF16),
            pltpu.VMEM((ts, mw), BF16),
            pltpu.VMEM((n_heads, hd, 2 * hd), F32),
            pltpu.VMEM((n_heads, 1, LANES), F32),
        ],
        compiler_params=pltpu.CompilerParams(
            dimension_semantics=("arbitrary", "arbitrary"), vmem_limit_bytes=min(vmem, VMEM_BYTES - (8 << 20))),
        name="mixer",
    )(main, main, main, main, gates, qkw, qkb, gbias, mhw, dww, dwb, cnw, cnb)


def _out_ffn_kernel(y_ref, x_ref, wo_ref, gpost_ref, gpre_ref, wg_ref, wu_ref, wd_ref, gfpost_ref,
                    out_ref, a_ref, *, ff_chunk):
    mix = jnp.dot(y_ref[...], wo_ref[...], preferred_element_type=F32)
    out_ref[...] = x_ref[...] + _rms(mix, gpost_ref[...])
    hn = _rms(out_ref[...], gpre_ref[...]).astype(BF16)
    d_ff = wg_ref.shape[1]
    for c in range(d_ff // ff_chunk):
        sl = slice(c * ff_chunk, (c + 1) * ff_chunk)
        g = jnp.dot(hn, wg_ref[:, sl], preferred_element_type=F32)
        u = jnp.dot(hn, wu_ref[:, sl], preferred_element_type=F32)
        a_ref[:, sl] = (g * _sigmoid(g) * u).astype(BF16)
    ff = jnp.dot(a_ref[...], wd_ref[...], preferred_element_type=F32)
    out_ref[...] = out_ref[...] + _rms(ff, gfpost_ref[...])


def _out_ffn(y2d, x2d, wo, gpost, gpre, wg, wu, wd, gfpost, *, tm):
    t, d = x2d.shape
    d_mix = y2d.shape[1]
    d_ff = wg.shape[1]
    assert d_ff % MXU_N == 0
    const = lambda i: (0, 0)
    resident = functools.partial(pl.BlockSpec, index_map=const, pipeline_mode=pl.Buffered(1))
    vmem = (2 * tm * d_mix * 2 + 4 * tm * d * 4 + (d_mix * d + 3 * d * d_ff) * 2 + tm * d_ff * 2
            + 8 * tm * d * 4)
    return pl.pallas_call(
        functools.partial(_out_ffn_kernel, ff_chunk=MXU_N),
        out_shape=jax.ShapeDtypeStruct((t, d), F32),
        grid=(t // tm,),
        in_specs=[
            pl.BlockSpec((tm, d_mix), lambda i: (i, 0)),
            pl.BlockSpec((tm, d), lambda i: (i, 0)),
            resident((d_mix, d)),
            pl.BlockSpec((1, d), const),
            pl.BlockSpec((1, d), const),
            resident((d, d_ff)),
            resident((d, d_ff)),
            resident((d_ff, d)),
            pl.BlockSpec((1, d), const),
        ],
        out_specs=pl.BlockSpec((tm, d), lambda i: (i, 0)),
        scratch_shapes=[pltpu.VMEM((tm, d_ff), BF16)],
        compiler_params=pltpu.CompilerParams(
            dimension_semantics=("arbitrary",), vmem_limit_bytes=min(vmem, VMEM_BYTES - (8 << 20))),
        name="out_ffn",
    )(y2d, x2d, wo, gpost, gpre, wg, wu, wd, gfpost)


def _pick_tile(n, target):
    t = min(n, target)
    while n % t:
        t //= 2
    return t


def _layer(h, ln_mix_pre, ln_mix_post, w_in, qk_conv_w, qk_conv_b, i_bias, f_bias, mh_norm_w,
           dw_conv_w, dw_conv_b, conv_norm_w, conv_norm_b, w_out, ln_ffn_pre, ln_ffn_post,
           w_gate, w_up, w_down):
    b, s, d = h.shape
    n_heads = i_bias.shape[0]
    mw = mh_norm_w.shape[0]
    n_gate = 2 * n_heads
    row = lambda a: a.reshape(1, -1).astype(F32)

    w_main = jnp.concatenate([w_in[:, :4 * mw], w_in[:, 4 * mw + n_gate:]], axis=1).astype(BF16)
    w_gates = jnp.pad(w_in[:, 4 * mw:4 * mw + n_gate], ((0, 0), (0, LANES - n_gate))).astype(BF16)
    gbias = jnp.pad(jnp.concatenate([i_bias, f_bias]), (0, LANES - n_gate)).reshape(1, LANES).astype(F32)

    x2d = h.reshape(b * s, d)
    tm = _pick_tile(b * s, 512)
    main, gates = _in_proj(x2d, row(ln_mix_pre), w_main, w_gates, tm=tm)

    ts = _pick_tile(s, 512)
    assert ts % CHUNK == 0
    y = _mixer(main.reshape(b, s, -1), gates.reshape(b, s, LANES),
               qk_conv_w.astype(F32), row(qk_conv_b), gbias, row(mh_norm_w),
               dw_conv_w.astype(F32), row(dw_conv_b), row(conv_norm_w), row(conv_norm_b),
               n_heads=n_heads, ts=ts)

    out = _out_ffn(y.reshape(b * s, -1), x2d, w_out.astype(BF16), row(ln_mix_post), row(ln_ffn_pre),
                   w_gate.astype(BF16), w_up.astype(BF16), w_down.astype(BF16), row(ln_ffn_post), tm=tm)
    return out.reshape(b, s, d)


def kernel(x, ln_mix_pre, ln_mix_post, w_in, qk_conv_w, qk_conv_b, i_bias, f_bias, mh_norm_w,
           dw_conv_w, dw_conv_b, conv_norm_w, conv_norm_b, w_out, ln_ffn_pre, ln_ffn_post,
           w_gate, w_up, w_down):
    h = x
    for l in range(ln_mix_pre.shape[0]):
        h = _layer(h, ln_mix_pre[l], ln_mix_post[l], w_in[l], qk_conv_w[l], qk_conv_b[l], i_bias[l],
                   f_bias[l], mh_norm_w[l], dw_conv_w[l], dw_conv_b[l], conv_norm_w[l], conv_norm_b[l],
                   w_out[l], ln_ffn_pre[l], ln_ffn_post[l], w_gate[l], w_up[l], w_down[l])
    return h
```

```python
import functools

import jax
import jax.numpy as jnp
from jax import lax
from jax.experimental import pallas as pl
from jax.experimental.pallas import tpu as pltpu

F32 = jnp.float32
BF16 = jnp.bfloat16

NORM_EPS = 1e-6
LANES = 128
SUBLANES = 8
MXU_N = 256
VMEM_BYTES = 64 * 1024 * 1024
CHUNK = 128
CONV_ROW_STRIDE = 4


def _rms(x, w):
    return x * lax.rsqrt(jnp.mean(x * x, axis=-1, keepdims=True) + NORM_EPS) * w


def _sigmoid(x):
    return 1.0 / (1.0 + jnp.exp(-x))


def _log_sigmoid(x):
    return -(jnp.maximum(-x, 0.0) + jnp.log1p(jnp.exp(-jnp.abs(x))))


def _group_norm_lanes(x, w, b=None):
    mu = jnp.mean(x, axis=-1, keepdims=True)
    xc = x - mu
    var = jnp.mean(xc * xc, axis=-1, keepdims=True)
    y = xc * lax.rsqrt(var + NORM_EPS) * w
    return y if b is None else y + b


def _in_proj_kernel(x_ref, g_ref, wm_ref, wg_ref, main_ref, gates_ref, vt_ref, *, n_chunk, v_chunk):
    xn = _rms(x_ref[...], g_ref[...]).astype(BF16)
    n_main = wm_ref.shape[1]
    for c in range(n_main // n_chunk):
        sl = slice(c * n_chunk, (c + 1) * n_chunk)
        res = jnp.dot(xn, wm_ref[:, sl], preferred_element_type=F32)
        main_ref[:, sl] = res.astype(BF16)
        if c == v_chunk:
            for j in range(vt_ref.shape[0]):
                vt_ref[j] = res[j * CHUNK:(j + 1) * CHUNK, :].T.astype(BF16)
    gates = jnp.dot(xn, wg_ref[...], preferred_element_type=F32)
    gates_ref[...] = gates.T[0:gates_ref.shape[0], :]


def _in_proj(x2d, g, w_main, w_gate, *, tm, mw):
    t, d = x2d.shape
    n_main = w_main.shape[1]
    const = lambda i: (0, 0)
    vmem = (2 * tm * d * 4 + 2 * tm * n_main * 2 + 2 * tm * LANES * 4 + 2 * tm * mw * 2
            + 2 * (d * n_main + d * LANES) * 2 + 6 * tm * d * 4)
    return pl.pallas_call(
        functools.partial(_in_proj_kernel, n_chunk=mw, v_chunk=2),
        out_shape=(jax.ShapeDtypeStruct((t, n_main), BF16), jax.ShapeDtypeStruct((SUBLANES, t), F32),
                   jax.ShapeDtypeStruct((t // CHUNK, mw, CHUNK), BF16)),
        grid=(t // tm,),
        in_specs=[
            pl.BlockSpec((tm, d), lambda i: (i, 0)),
            pl.BlockSpec((1, d), const),
            pl.BlockSpec((d, n_main), const),
            pl.BlockSpec((d, LANES), const),
        ],
        out_specs=(pl.BlockSpec((tm, n_main), lambda i: (i, 0)), pl.BlockSpec((SUBLANES, tm), lambda i: (0, i)),
                   pl.BlockSpec((tm // CHUNK, mw, CHUNK), lambda i: (i, 0, 0))),
        compiler_params=pltpu.CompilerParams(
            dimension_semantics=("arbitrary",), vmem_limit_bytes=min(vmem, VMEM_BYTES - (8 << 20))),
        name="in_proj",
    )(x2d, g, w_main, w_gate)


def _prefix_max_rows(x):
    sub = lax.broadcasted_iota(jnp.int32, (SUBLANES, x.shape[1]), 0)
    out, run = [], None
    for j in range(x.shape[0] // SUBLANES):
        blk = x[j * SUBLANES:(j + 1) * SUBLANES, :]
        shift = 1
        while shift < SUBLANES:
            blk = jnp.maximum(blk, jnp.where(sub >= shift, pltpu.roll(blk, shift, 0), -jnp.inf))
            shift *= 2
        if run is not None:
            blk = jnp.maximum(blk, run)
        run = jnp.broadcast_to(blk[SUBLANES - 1:SUBLANES, :], blk.shape)
        out.append(blk)
    return jnp.concatenate(out, axis=0)


def _mixer_kernel(qk_ref, v_ref, o_ref, cvcg_ref, gates_ref, vt_ref,
                  qkw_ref, qkb_ref, gbias_ref, mhw_ref, dww_ref, dwb_ref, cnw_ref, cnb_ref,
                  y_ref,
                  qkext_ref, qkact_ref, uext_ref, yconv_ref, ga_ref, gb_ref, gcm_ref, gbt_ref,
                  c_ref, m_ref, mlane_ref,
                  *, n_heads, qk_taps, dw_taps, row_blk):
    ts = qk_ref.shape[0]
    mw = v_ref.shape[1]
    cw = cvcg_ref.shape[1] // 2
    hd = mw // n_heads
    L = CHUNK
    P = CONV_ROW_STRIDE
    n_ph = ts // P
    qk_pad = qkext_ref.shape[1] - ts
    u_pad = uext_ref.shape[1] - ts
    n_qk_slabs, n_u_slabs = 2 * mw // LANES, cw // LANES

    @pl.when(pl.program_id(1) == 0)
    def _():
        qkext_ref[:, 0:qk_pad, :] = jnp.zeros((n_qk_slabs, qk_pad, LANES), F32)
        uext_ref[:, 0:u_pad, :] = jnp.zeros((n_u_slabs, u_pad, LANES), F32)
        c_ref[...] = jnp.zeros(c_ref.shape, F32)
        m_ref[...] = jnp.zeros(m_ref.shape, F32)
        mlane_ref[...] = jnp.zeros(mlane_ref.shape, F32)

    for blk in range(ts // row_blk):
        rows = slice(blk * row_blk, (blk + 1) * row_blk)
        for g in range(n_qk_slabs):
            qkext_ref[g, qk_pad + rows.start:qk_pad + rows.stop, :] = (
                qk_ref[rows, g * LANES:(g + 1) * LANES].astype(F32))
        for g in range(n_u_slabs):
            cv = cvcg_ref[rows, g * LANES:(g + 1) * LANES].astype(F32)
            cg = cvcg_ref[rows, cw + g * LANES:cw + (g + 1) * LANES].astype(F32)
            uext_ref[g, u_pad + rows.start:u_pad + rows.stop, :] = cv * _sigmoid(cg)

    row_i = lax.broadcasted_iota(jnp.int32, (L, L), 0)
    col_i = lax.broadcasted_iota(jnp.int32, (L, L), 1)
    causal = row_i >= col_i
    triu = jnp.where(row_i <= col_i, 1.0, 0.0).astype(BF16)
    head_row = lax.broadcasted_iota(jnp.int32, (SUBLANES, L), 0) < n_heads
    pad_rows = jnp.zeros((LANES - SUBLANES, L), F32)
    for c in range(ts // L):
        g_pre = gates_ref[:, c * L:(c + 1) * L] + gbias_ref[...]
        logf = _log_sigmoid(g_pre)
        hi = logf.astype(BF16)
        r1 = logf - hi.astype(F32)
        mid = r1.astype(BF16)
        lo = (r1 - mid.astype(F32)).astype(BF16)
        b_all = (jnp.dot(lo, triu, preferred_element_type=F32) + jnp.dot(mid, triu, preferred_element_type=F32)
                 + jnp.dot(hi, triu, preferred_element_type=F32))
        b = jnp.where(head_row, pltpu.roll(b_all, SUBLANES - n_heads, 0), 0.0)
        a = jnp.where(head_row, g_pre - b, 0.0)
        ga_ref[c] = a
        gb_ref[c] = b
        gcm_ref[c] = _prefix_max_rows(jnp.concatenate([a, pad_rows], axis=0).T)
        gbt_ref[c] = jnp.concatenate([b, pad_rows], axis=0).T

    q_scale = hd ** -0.5

    def qk_taps_of(idx):
        g, ph = idx // P, idx % P
        acc = jnp.broadcast_to(qkb_ref[g], (n_ph, LANES))
        for k in range(qk_taps):
            start = qk_pad - (qk_taps - 1) + k + ph
            acc = acc + qkext_ref[g, pl.ds(start, n_ph, stride=P), :] * qkw_ref[g, k:k + 1, :]
        return acc

    def qk_finish(idx, acc):
        g, ph = idx // P, idx % P
        scale = jnp.where(g < n_qk_slabs // 2, q_scale, 1.0)
        qkact_ref[g, pl.ds(ph, n_ph, stride=P), :] = acc * _sigmoid(acc) * scale

    for idx in range(n_qk_slabs * P):
        qk_finish(idx, qk_taps_of(idx))

    def dw_taps_of(idx):
        g, ph = idx // P, idx % P
        acc = jnp.broadcast_to(dwb_ref[g], (n_ph, LANES))
        for k in range(dw_taps):
            start = u_pad - (dw_taps - 1) + k + ph
            acc = acc + uext_ref[g, pl.ds(start, n_ph, stride=P), :] * dww_ref[g, k:k + 1, :]
        return acc

    def dw_finish(idx, acc):
        g, ph = idx // P, idx % P
        mu = jnp.mean(acc, axis=-1, keepdims=True)
        var = jnp.maximum(jnp.mean(acc * acc, axis=-1, keepdims=True) - mu * mu, 0.0)
        yn = (acc - mu) * lax.rsqrt(var + NORM_EPS) * cnw_ref[g] + cnb_ref[g]
        yconv_ref[g, pl.ds(ph, n_ph, stride=P), :] = yn * _sigmoid(yn)

    for idx in range(n_u_slabs * P):
        dw_finish(idx, dw_taps_of(idx))

    for g in range(n_qk_slabs):
        qkext_ref[g, 0:qk_pad, :] = qkext_ref[g, ts:ts + qk_pad, :]
    for g in range(n_u_slabs):
        uext_ref[g, 0:u_pad, :] = uext_ref[g, ts:ts + u_pad, :]

    ones_blk = jnp.ones((L, hd), BF16)
    nt = (((1,), (1,)), ((), ()))

    def chunk_step(c, carry):
        rows = pl.ds(pl.multiple_of(c * L, L), L)
        a, b = ga_ref[c], gb_ref[c]
        m_prev = m_ref[...]
        m_end = jnp.maximum(m_prev, jnp.max(a, axis=1, keepdims=True))
        decay = jnp.exp(m_prev - m_end)
        wa = jnp.exp(a - m_end)
        m_ref[...] = b[:, L - 1:L] + m_end
        mm_all = jnp.maximum(mlane_ref[...], gcm_ref[c])
        m_all = gbt_ref[c] + mm_all
        mlane_ref[...] = m_all[L - 1:L, :]

        for h in range(n_heads):
            hl = slice(h * hd, (h + 1) * hd)
            mm_col = mm_all[:, h:h + 1]
            p = jnp.where(causal, jnp.exp(a[h:h + 1, :] - mm_col), 0.0)
            w_inter = jnp.exp(m_prev[h:h + 1, :] - mm_col)
            em = jnp.exp(-jnp.broadcast_to(m_all[:, h:h + 1], (L, LANES)))

            qc = qkact_ref[h, rows, :].astype(BF16)
            kc = qkact_ref[n_heads + h, rows, :].astype(BF16)
            s = lax.dot_general(qc, kc, nt, preferred_element_type=F32) * p
            v_aug = jnp.concatenate([v_ref[rows, hl], ones_blk], axis=1)
            ct = c_ref[h]
            r = (jnp.concatenate([w_inter, w_inter], axis=1)
                 * lax.dot_general(qc, ct.astype(BF16), nt, preferred_element_type=F32)
                 + jnp.dot(s.astype(BF16), v_aug, preferred_element_type=F32))
            cell = r[:, :hd] * (1.0 / jnp.maximum(jnp.abs(r[:, hd:]), em))

            wa_h = wa[h:h + 1, :]
            vtw = jnp.concatenate([vt_ref[c, hl, :].astype(F32) * wa_h, jnp.broadcast_to(wa_h, (hd, L))], axis=0)
            c_ref[h] = decay[h:h + 1, :] * ct + jnp.dot(vtw.astype(BF16), kc, preferred_element_type=F32)

            yn = _group_norm_lanes(cell, mhw_ref[:, hl])
            y_ref[rows, hl] = (_sigmoid(o_ref[rows, hl].astype(F32)) * yn).astype(BF16)

        for g in range(n_u_slabs):
            y_ref[rows, mw + g * LANES:mw + (g + 1) * LANES] = yconv_ref[g, rows, :].astype(BF16)
        return carry

    lax.fori_loop(0, ts // L, chunk_step, 0, unroll=2)


def _mixer(main, gates, vt, qkw, qkb, gbias, mhw, dww, dwb, cnw, cnb, *, n_heads, ts):
    b, s, n_main = main.shape
    mw = mhw.shape[1]
    n_u_slabs = dww.shape[0]
    cw = n_u_slabs * LANES
    hd = mw // n_heads
    assert hd == LANES and CHUNK == hd and n_main == 4 * mw + 2 * cw and cw == mw and n_heads <= SUBLANES // 2
    qk_taps, dw_taps = qkw.shape[1], dww.shape[1]
    qk_pad = -(-(qk_taps - 1) // SUBLANES) * SUBLANES
    u_pad = -(-(dw_taps - 1) // SUBLANES) * SUBLANES
    const2 = lambda i, j: (0, 0)
    const3 = lambda i, j: (0, 0, 0)
    kern = functools.partial(_mixer_kernel, n_heads=n_heads, qk_taps=qk_taps, dw_taps=dw_taps, row_blk=64)
    vmem = (2 * ts * (2 * mw + 2 * mw + mw + 2 * cw) * 2 + 2 * ts * SUBLANES * 4 + 2 * ts * (mw + cw) * 2
            + (2 * ts + qk_pad) * 2 * mw * 4 + (2 * ts + u_pad) * cw * 4
            + n_heads * hd * 2 * hd * 4 + (16 << 20))
    return pl.pallas_call(
        kern,
        out_shape=jax.ShapeDtypeStruct((b, s, mw + cw), BF16),
        grid=(b, s // ts),
        in_specs=[
            pl.BlockSpec((None, ts, 2 * mw), lambda i, j: (i, j, 0)),
            pl.BlockSpec((None, ts, mw), lambda i, j: (i, j, 2)),
            pl.BlockSpec((None, ts, mw), lambda i, j: (i, j, 3)),
            pl.BlockSpec((None, ts, 2 * cw), lambda i, j: (i, j, 2)),
            pl.BlockSpec((SUBLANES, ts), lambda i, j: (0, i * (s // ts) + j)),
            pl.BlockSpec((ts // CHUNK, mw, CHUNK), lambda i, j: (i * (s // ts) + j, 0, 0)),
            pl.BlockSpec(qkw.shape, const3),
            pl.BlockSpec(qkb.shape, const3),
            pl.BlockSpec((SUBLANES, CHUNK), const2),
            pl.BlockSpec((1, mw), const2),
            pl.BlockSpec(dww.shape, const3),
            pl.BlockSpec(dwb.shape, const3),
            pl.BlockSpec(cnw.shape, const3),
            pl.BlockSpec(cnb.shape, const3),
        ],
        out_specs=pl.BlockSpec((None, ts, mw + cw), lambda i, j: (i, j, 0)),
        scratch_shapes=[
            pltpu.VMEM((2 * mw // LANES, ts + qk_pad, LANES), F32),
            pltpu.VMEM((2 * mw // LANES, ts, LANES), F32),
            pltpu.VMEM((n_u_slabs, ts + u_pad, LANES), F32),
            pltpu.VMEM((n_u_slabs, ts, LANES), F32),
            pltpu.VMEM((ts // CHUNK, SUBLANES, CHUNK), F32),
            pltpu.VMEM((ts // CHUNK, SUBLANES, CHUNK), F32),
            pltpu.VMEM((ts // CHUNK, CHUNK, LANES), F32),
            pltpu.VMEM((ts // CHUNK, CHUNK, LANES), F32),
            pltpu.VMEM((n_heads, 2 * hd, hd), F32),
            pltpu.VMEM((SUBLANES, CHUNK), F32),
            pltpu.VMEM((1, LANES), F32),
        ],
        compiler_params=pltpu.CompilerParams(
            dimension_semantics=("arbitrary", "arbitrary"), vmem_limit_bytes=min(vmem, VMEM_BYTES - (8 << 20))),
        name="mixer",
    )(main, main, main, main, gates, vt, qkw, qkb, gbias, mhw, dww, dwb, cnw, cnb)


def _out_ffn_kernel(y_ref, x_ref, wo_ref, gpost_ref, gpre_ref, wg_ref, wu_ref, wd_ref, gfpost_ref,
                    out_ref, a_ref, *, ff_chunk):
    mix = jnp.dot(y_ref[...], wo_ref[...], preferred_element_type=F32)
    out_ref[...] = x_ref[...] + _rms(mix, gpost_ref[...])
    hn = _rms(out_ref[...], gpre_ref[...]).astype(BF16)
    d_ff = wg_ref.shape[1]
    for c in range(d_ff // ff_chunk):
        sl = slice(c * ff_chunk, (c + 1) * ff_chunk)
        g = jnp.dot(hn, wg_ref[:, sl], preferred_element_type=F32)
        u = jnp.dot(hn, wu_ref[:, sl], preferred_element_type=F32)
        a_ref[:, sl] = (g * _sigmoid(g) * u).astype(BF16)
    ff = jnp.dot(a_ref[...], wd_ref[...], preferred_element_type=F32)
    out_ref[...] = out_ref[...] + _rms(ff, gfpost_ref[...])


def _out_ffn(y2d, x2d, wo, gpost, gpre, wg, wu, wd, gfpost, *, tm):
    t, d = x2d.shape
    d_mix = y2d.shape[1]
    d_ff = wg.shape[1]
    assert d_ff % MXU_N == 0
    const = lambda i: (0, 0)
    resident = functools.partial(pl.BlockSpec, index_map=const, pipeline_mode=pl.Buffered(1))
    vmem = (2 * tm * d_mix * 2 + 4 * tm * d * 4 + (d_mix * d + 3 * d * d_ff) * 2 + tm * d_ff * 2
            + 8 * tm * d * 4)
    return pl.pallas_call(
        functools.partial(_out_ffn_kernel, ff_chunk=MXU_N),
        out_shape=jax.ShapeDtypeStruct((t, d), F32),
        grid=(t // tm,),
        in_specs=[
            pl.BlockSpec((tm, d_mix), lambda i: (i, 0)),
            pl.BlockSpec((tm, d), lambda i: (i, 0)),
            resident((d_mix, d)),
            pl.BlockSpec((1, d), const),
            pl.BlockSpec((1, d), const),
            resident((d, d_ff)),
            resident((d, d_ff)),
            resident((d_ff, d)),
            pl.BlockSpec((1, d), const),
        ],
        out_specs=pl.BlockSpec((tm, d), lambda i: (i, 0)),
        scratch_shapes=[pltpu.VMEM((tm, d_ff), BF16)],
        compiler_params=pltpu.CompilerParams(
            dimension_semantics=("arbitrary",), vmem_limit_bytes=min(vmem, VMEM_BYTES - (8 << 20))),
        name="out_ffn",
    )(y2d, x2d, wo, gpost, gpre, wg, wu, wd, gfpost)


def _pick_tile(n, target):
    t = min(n, target)
    while n % t:
        t //= 2
    return t


def _slabs(a):
    a = a.astype(F32).reshape(a.shape[0] if a.ndim == 2 else 1, -1, LANES)
    return jnp.transpose(a, (1, 0, 2))


def _layer(h, ln_mix_pre, ln_mix_post, w_in, qk_conv_w, qk_conv_b, i_bias, f_bias, mh_norm_w,
           dw_conv_w, dw_conv_b, conv_norm_w, conv_norm_b, w_out, ln_ffn_pre, ln_ffn_post,
           w_gate, w_up, w_down):
    b, s, d = h.shape
    n_heads = i_bias.shape[0]
    mw = mh_norm_w.shape[0]
    n_gate = 2 * n_heads
    row = lambda a: a.reshape(1, -1).astype(F32)

    w_main = jnp.concatenate([w_in[:, :4 * mw], w_in[:, 4 * mw + n_gate:]], axis=1).astype(BF16)
    w_gates = jnp.pad(w_in[:, 4 * mw:4 * mw + n_gate], ((0, 0), (0, LANES - n_gate))).astype(BF16)
    gbias = jnp.broadcast_to(
        jnp.pad(jnp.concatenate([i_bias, f_bias]), (0, SUBLANES - n_gate)).reshape(SUBLANES, 1).astype(F32),
        (SUBLANES, CHUNK))

    x2d = h.reshape(b * s, d)
    tm = _pick_tile(b * s, 512)
    main, gates, vt = _in_proj(x2d, row(ln_mix_pre), w_main, w_gates, tm=tm, mw=mw)

    ts = _pick_tile(s, 512)
    assert ts % CHUNK == 0
    y = _mixer(main.reshape(b, s, -1), gates, vt,
               _slabs(qk_conv_w), _slabs(qk_conv_b), gbias, row(mh_norm_w),
               _slabs(dw_conv_w), _slabs(dw_conv_b), _slabs(conv_norm_w), _slabs(conv_norm_b),
               n_heads=n_heads, ts=ts)

    out = _out_ffn(y.reshape(b * s, -1), x2d, w_out.astype(BF16), row(ln_mix_post), row(ln_ffn_pre),
                   w_gate.astype(BF16), w_up.astype(BF16), w_down.astype(BF16), row(ln_ffn_post), tm=tm)
    return out.reshape(b, s, d)


def kernel(x, ln_mix_pre, ln_mix_post, w_in, qk_conv_w, qk_conv_b, i_bias, f_bias, mh_norm_w,
           dw_conv_w, dw_conv_b, conv_norm_w, conv_norm_b, w_out, ln_ffn_pre, ln_ffn_post,
           w_gate, w_up, w_down):
    h = x
    for l in range(ln_mix_pre.shape[0]):
        h = _layer(h, ln_mix_pre[l], ln_mix_post[l], w_in[l], qk_conv_w[l], qk_conv_b[l], i_bias[l],
                   f_bias[l], mh_norm_w[l], dw_conv_w[l], dw_conv_b[l], conv_norm_w[l], conv_norm_b[l],
                   w_out[l], ln_ffn_pre[l], ln_ffn_post[l], w_gate[l], w_up[l], w_down[l])
    return h
```

```python
import functools

import jax
import jax.numpy as jnp
from jax import lax
from jax.experimental import pallas as pl
from jax.experimental.pallas import tpu as pltpu

F32 = jnp.float32
BF16 = jnp.bfloat16

NORM_EPS = 1e-6
LANES = 128
SUBLANES = 8
MXU_N = 256
VMEM_BYTES = 64 * 1024 * 1024
CHUNK = 128
CONV_ROW_STRIDE = 4
CONV_BLOCK_ROWS = 32


def _rms(x, w):
    return x * lax.rsqrt(jnp.mean(x * x, axis=-1, keepdims=True) + NORM_EPS) * w


def _sigmoid(x):
    return 1.0 / (1.0 + jnp.exp(-x))


def _log_sigmoid(x):
    return -(jnp.maximum(-x, 0.0) + jnp.log1p(jnp.exp(-jnp.abs(x))))


def _group_norm_lanes(x, w, b=None):
    mu = jnp.mean(x, axis=-1, keepdims=True)
    xc = x - mu
    var = jnp.mean(xc * xc, axis=-1, keepdims=True)
    y = xc * lax.rsqrt(var + NORM_EPS) * w
    return y if b is None else y + b


def _in_proj_kernel(x_ref, g_ref, wm_ref, wg_ref, main_ref, gates_ref, vt_ref, *, n_chunk, v_chunk):
    xn = _rms(x_ref[...], g_ref[...]).astype(BF16)
    n_main = wm_ref.shape[1]
    for c in range(n_main // n_chunk):
        sl = slice(c * n_chunk, (c + 1) * n_chunk)
        res = jnp.dot(xn, wm_ref[:, sl], preferred_element_type=F32)
        main_ref[:, sl] = res.astype(BF16)
        if c == v_chunk:
            for j in range(vt_ref.shape[0]):
                vt_ref[j] = res[j * CHUNK:(j + 1) * CHUNK, :].T.astype(BF16)
    gates = jnp.dot(xn, wg_ref[...], preferred_element_type=F32)
    gates_ref[...] = gates.T[0:gates_ref.shape[0], :]


def _in_proj(x2d, g, w_main, w_gate, *, tm, mw):
    t, d = x2d.shape
    n_main = w_main.shape[1]
    const = lambda i: (0, 0)
    vmem = (2 * tm * d * 4 + 2 * tm * n_main * 2 + 2 * tm * LANES * 4 + 2 * tm * mw * 2
            + 2 * (d * n_main + d * LANES) * 2 + 6 * tm * d * 4)
    return pl.pallas_call(
        functools.partial(_in_proj_kernel, n_chunk=mw, v_chunk=2),
        out_shape=(jax.ShapeDtypeStruct((t, n_main), BF16), jax.ShapeDtypeStruct((SUBLANES, t), F32),
                   jax.ShapeDtypeStruct((t // CHUNK, mw, CHUNK), BF16)),
        grid=(t // tm,),
        in_specs=[
            pl.BlockSpec((tm, d), lambda i: (i, 0)),
            pl.BlockSpec((1, d), const),
            pl.BlockSpec((d, n_main), const),
            pl.BlockSpec((d, LANES), const),
        ],
        out_specs=(pl.BlockSpec((tm, n_main), lambda i: (i, 0)), pl.BlockSpec((SUBLANES, tm), lambda i: (0, i)),
                   pl.BlockSpec((tm // CHUNK, mw, CHUNK), lambda i: (i, 0, 0))),
        compiler_params=pltpu.CompilerParams(
            dimension_semantics=("arbitrary",), vmem_limit_bytes=min(vmem, VMEM_BYTES - (8 << 20))),
        name="in_proj",
    )(x2d, g, w_main, w_gate)


def _prefix_max_rows(x):
    sub = lax.broadcasted_iota(jnp.int32, (SUBLANES, x.shape[1]), 0)
    out, run = [], None
    for j in range(x.shape[0] // SUBLANES):
        blk = x[j * SUBLANES:(j + 1) * SUBLANES, :]
        shift = 1
        while shift < SUBLANES:
            blk = jnp.maximum(blk, jnp.where(sub >= shift, pltpu.roll(blk, shift, 0), -jnp.inf))
            shift *= 2
        if run is not None:
            blk = jnp.maximum(blk, run)
        run = jnp.broadcast_to(blk[SUBLANES - 1:SUBLANES, :], blk.shape)
        out.append(blk)
    return jnp.concatenate(out, axis=0)


def _mixer_kernel(qk_ref, v_ref, o_ref, gates_ref, vt_ref,
                  qkw_ref, qkb_ref, gbias_ref, mhw_ref,
                  y_ref,
                  qkext_ref, qkact_ref, ga_ref, gb_ref, gcm_ref, gbt_ref,
                  c_ref, m_ref, mlane_ref,
                  *, n_heads, qk_taps, row_blk):
    ts = qk_ref.shape[0]
    mw = v_ref.shape[1]
    hd = mw // n_heads
    L = CHUNK
    P = CONV_ROW_STRIDE
    n_ph = ts // P
    qk_pad = qkext_ref.shape[1] - ts
    n_qk_slabs = 2 * mw // LANES

    @pl.when(pl.program_id(1) == 0)
    def _():
        qkext_ref[:, 0:qk_pad, :] = jnp.zeros((n_qk_slabs, qk_pad, LANES), F32)
        c_ref[...] = jnp.zeros(c_ref.shape, F32)
        m_ref[...] = jnp.zeros(m_ref.shape, F32)
        mlane_ref[...] = jnp.zeros(mlane_ref.shape, F32)

    for blk in range(ts // row_blk):
        rows = slice(blk * row_blk, (blk + 1) * row_blk)
        for g in range(n_qk_slabs):
            qkext_ref[g, qk_pad + rows.start:qk_pad + rows.stop, :] = (
                qk_ref[rows, g * LANES:(g + 1) * LANES].astype(F32))

    row_i = lax.broadcasted_iota(jnp.int32, (L, L), 0)
    col_i = lax.broadcasted_iota(jnp.int32, (L, L), 1)
    causal = row_i >= col_i
    triu = jnp.where(row_i <= col_i, 1.0, 0.0).astype(BF16)
    head_row = lax.broadcasted_iota(jnp.int32, (SUBLANES, L), 0) < n_heads
    pad_rows = jnp.zeros((LANES - SUBLANES, L), F32)
    for c in range(ts // L):
        g_pre = gates_ref[:, c * L:(c + 1) * L] + gbias_ref[...]
        logf = _log_sigmoid(g_pre)
        hi = logf.astype(BF16)
        r1 = logf - hi.astype(F32)
        mid = r1.astype(BF16)
        lo = (r1 - mid.astype(F32)).astype(BF16)
        b_all = (jnp.dot(lo, triu, preferred_element_type=F32) + jnp.dot(mid, triu, preferred_element_type=F32)
                 + jnp.dot(hi, triu, preferred_element_type=F32))
        b = jnp.where(head_row, pltpu.roll(b_all, SUBLANES - n_heads, 0), 0.0)
        a = jnp.where(head_row, g_pre - b, 0.0)
        ga_ref[c] = a
        gb_ref[c] = b
        gcm_ref[c] = _prefix_max_rows(jnp.concatenate([a, pad_rows], axis=0).T)
        gbt_ref[c] = jnp.concatenate([b, pad_rows], axis=0).T

    q_scale = hd ** -0.5

    def qk_taps_of(idx):
        g, ph = idx // P, idx % P
        acc = jnp.broadcast_to(qkb_ref[g], (n_ph, LANES))
        for k in range(qk_taps):
            start = qk_pad - (qk_taps - 1) + k + ph
            acc = acc + qkext_ref[g, pl.ds(start, n_ph, stride=P), :] * qkw_ref[g, k:k + 1, :]
        return acc

    def qk_finish(idx, acc):
        g, ph = idx // P, idx % P
        scale = jnp.where(g < n_qk_slabs // 2, q_scale, 1.0)
        qkact_ref[g, pl.ds(ph, n_ph, stride=P), :] = acc * _sigmoid(acc) * scale

    for idx in range(n_qk_slabs * P):
        qk_finish(idx, qk_taps_of(idx))

    for g in range(n_qk_slabs):
        qkext_ref[g, 0:qk_pad, :] = qkext_ref[g, ts:ts + qk_pad, :]

    ones_blk = jnp.ones((L, hd), BF16)
    nt = (((1,), (1,)), ((), ()))

    def chunk_step(c, carry):
        rows = pl.ds(pl.multiple_of(c * L, L), L)
        a, b = ga_ref[c], gb_ref[c]
        m_prev = m_ref[...]
        m_end = jnp.maximum(m_prev, jnp.max(a, axis=1, keepdims=True))
        decay = jnp.exp(m_prev - m_end)
        wa = jnp.exp(a - m_end)
        m_ref[...] = b[:, L - 1:L] + m_end
        mm_all = jnp.maximum(mlane_ref[...], gcm_ref[c])
        m_all = gbt_ref[c] + mm_all
        mlane_ref[...] = m_all[L - 1:L, :]

        for h in range(n_heads):
            hl = slice(h * hd, (h + 1) * hd)
            mm_col = mm_all[:, h:h + 1]
            p = jnp.where(causal, jnp.exp(a[h:h + 1, :] - mm_col), 0.0)
            w_inter = jnp.exp(m_prev[h:h + 1, :] - mm_col)
            em = jnp.exp(-jnp.broadcast_to(m_all[:, h:h + 1], (L, LANES)))

            qc = qkact_ref[h, rows, :].astype(BF16)
            kc = qkact_ref[n_heads + h, rows, :].astype(BF16)
            s = lax.dot_general(qc, kc, nt, preferred_element_type=F32) * p
            v_aug = jnp.concatenate([v_ref[rows, hl], ones_blk], axis=1)
            ct = c_ref[h]
            r = (jnp.concatenate([w_inter, w_inter], axis=1)
                 * lax.dot_general(qc, ct.astype(BF16), nt, preferred_element_type=F32)
                 + jnp.dot(s.astype(BF16), v_aug, preferred_element_type=F32))
            cell = r[:, :hd] * (1.0 / jnp.maximum(jnp.abs(r[:, hd:]), em))

            wa_h = wa[h:h + 1, :]
            vtw = jnp.concatenate([vt_ref[c, hl, :].astype(F32) * wa_h, jnp.broadcast_to(wa_h, (hd, L))], axis=0)
            c_ref[h] = decay[h:h + 1, :] * ct + jnp.dot(vtw.astype(BF16), kc, preferred_element_type=F32)

            yn = _group_norm_lanes(cell, mhw_ref[:, hl])
            y_ref[rows, hl] = (_sigmoid(o_ref[rows, hl].astype(F32)) * yn).astype(BF16)

        return carry

    lax.fori_loop(0, ts // L, chunk_step, 0, unroll=2)


def _mixer(main, gates, vt, qkw, qkb, gbias, mhw, *, n_heads, ts):
    b, s, n_main = main.shape
    mw = mhw.shape[1]
    hd = mw // n_heads
    assert hd == LANES and CHUNK == hd and n_main >= 4 * mw and n_heads <= SUBLANES // 2
    qk_taps = qkw.shape[1]
    qk_pad = -(-(qk_taps - 1) // SUBLANES) * SUBLANES
    const2 = lambda i, j: (0, 0)
    const3 = lambda i, j: (0, 0, 0)
    kern = functools.partial(_mixer_kernel, n_heads=n_heads, qk_taps=qk_taps, row_blk=64)
    vmem = (2 * ts * (2 * mw + 2 * mw + mw) * 2 + 2 * ts * SUBLANES * 4 + 2 * ts * mw * 2
            + (2 * ts + qk_pad) * 2 * mw * 4
            + n_heads * hd * 2 * hd * 4 + (16 << 20))
    return pl.pallas_call(
        kern,
        out_shape=jax.ShapeDtypeStruct((b, s, mw), BF16),
        grid=(b, s // ts),
        in_specs=[
            pl.BlockSpec((None, ts, 2 * mw), lambda i, j: (i, j, 0)),
            pl.BlockSpec((None, ts, mw), lambda i, j: (i, j, 2)),
            pl.BlockSpec((None, ts, mw), lambda i, j: (i, j, 3)),
            pl.BlockSpec((SUBLANES, ts), lambda i, j: (0, i * (s // ts) + j)),
            pl.BlockSpec((ts // CHUNK, mw, CHUNK), lambda i, j: (i * (s // ts) + j, 0, 0)),
            pl.BlockSpec(qkw.shape, const3),
            pl.BlockSpec(qkb.shape, const3),
            pl.BlockSpec((SUBLANES, CHUNK), const2),
            pl.BlockSpec((1, mw), const2),
        ],
        out_specs=pl.BlockSpec((None, ts, mw), lambda i, j: (i, j, 0)),
        scratch_shapes=[
            pltpu.VMEM((2 * mw // LANES, ts + qk_pad, LANES), F32),
            pltpu.VMEM((2 * mw // LANES, ts, LANES), F32),
            pltpu.VMEM((ts // CHUNK, SUBLANES, CHUNK), F32),
            pltpu.VMEM((ts // CHUNK, SUBLANES, CHUNK), F32),
            pltpu.VMEM((ts // CHUNK, CHUNK, LANES), F32),
            pltpu.VMEM((ts // CHUNK, CHUNK, LANES), F32),
            pltpu.VMEM((n_heads, 2 * hd, hd), F32),
            pltpu.VMEM((SUBLANES, CHUNK), F32),
            pltpu.VMEM((1, LANES), F32),
        ],
        compiler_params=pltpu.CompilerParams(
            dimension_semantics=("arbitrary", "arbitrary"), vmem_limit_bytes=min(vmem, VMEM_BYTES - (8 << 20))),
        name="mixer",
    )(main, main, main, gates, vt, qkw, qkb, gbias, mhw)


def _out_ffn_kernel(ym_ref, cvcg_ref, x_ref, wo_ref, gpost_ref, gpre_ref, wg_ref, wu_ref, wd_ref, gfpost_ref,
                    dww_ref, dwb_ref, cnw_ref, cnb_ref,
                    out_ref, a_ref, uext_ref, yst_ref, yc_ref,
                    *, ff_chunk, dw_taps, tiles_per_seq, row_blk):
    i = pl.program_id(0)
    tm = x_ref.shape[0]
    mw = ym_ref.shape[1]
    cw = cvcg_ref.shape[1] // 2
    P, n_blk = CONV_ROW_STRIDE, CONV_BLOCK_ROWS
    u_pad = uext_ref.shape[2] - tm
    n_u_slabs = cw // LANES
    slot = i % 2
    prev = 1 - slot

    @pl.when(i == 0)
    def _():
        uext_ref[1] = jnp.zeros(uext_ref.shape[1:], F32)
        yc_ref[1] = jnp.zeros(yc_ref.shape[1:], BF16)

    seq_start = (jnp.minimum(i, pl.num_programs(0) - 2) % tiles_per_seq) == 0
    for g in range(n_u_slabs):
        uext_ref[slot, g, 0:u_pad, :] = jnp.where(seq_start, 0.0, uext_ref[prev, g, tm:tm + u_pad, :])
    for blk in range(tm // row_blk):
        rows = slice(blk * row_blk, (blk + 1) * row_blk)
        for g in range(n_u_slabs):
            cv = cvcg_ref[rows, g * LANES:(g + 1) * LANES].astype(F32)
            cg = cvcg_ref[rows, cw + g * LANES:cw + (g + 1) * LANES].astype(F32)
            uext_ref[slot, g, u_pad + rows.start:u_pad + rows.stop, :] = cv * _sigmoid(cg)

    def dw_conv(g, ph, r0):
        acc = jnp.broadcast_to(dwb_ref[g], (n_blk, LANES))
        for k in range(dw_taps):
            start = u_pad - (dw_taps - 1) + k + ph + r0
            acc = acc + uext_ref[slot, g, pl.ds(start, n_blk, stride=P), :] * dww_ref[g, k:k + 1, :]
        mu = jnp.mean(acc, axis=-1, keepdims=True)
        var = jnp.maximum(jnp.mean(acc * acc, axis=-1, keepdims=True) - mu * mu, 0.0)
        yn = (acc - mu) * lax.rsqrt(var + NORM_EPS) * cnw_ref[g] + cnb_ref[g]
        y = yn * _sigmoid(yn)
        yst_ref[g, pl.ds(ph + r0, n_blk, stride=P), :] = y
        return y

    conv_pieces = [(g, ph, r0) for g in range(n_u_slabs) for ph in range(P) for r0 in range(0, tm, n_blk * P)]

    mix = (jnp.dot(ym_ref[...], wo_ref[0:mw, :], preferred_element_type=F32)
           + jnp.dot(yc_ref[prev], wo_ref[mw:, :], preferred_element_type=F32))
    out_ref[...] = x_ref[...] + _rms(mix, gpost_ref[...])
    hn = _rms(out_ref[...], gpre_ref[...]).astype(BF16)
    d_ff = wg_ref.shape[1]
    n_chunks = d_ff // ff_chunk
    per_chunk = -(-len(conv_pieces) // n_chunks)
    tie = None
    for c in range(n_chunks):
        sl = slice(c * ff_chunk, (c + 1) * ff_chunk)
        g = jnp.dot(hn, wg_ref[:, sl], preferred_element_type=F32)
        u = jnp.dot(hn, wu_ref[:, sl], preferred_element_type=F32)
        act = g * _sigmoid(g) * u
        if tie is not None:
            act = act + tie
        a_ref[:, sl] = act.astype(BF16)
        tie = None
        for piece in conv_pieces[c * per_chunk:(c + 1) * per_chunk]:
            z = jnp.where(dw_conv(*piece)[0:1, :] > jnp.inf, 1.0, 0.0)
            tie = z if tie is None else tie + z
        if tie is not None:
            tie = jnp.concatenate([tie] * (ff_chunk // LANES), axis=1)
    for g in range(n_u_slabs):
        yc_ref[slot, :, g * LANES:(g + 1) * LANES] = yst_ref[g].astype(BF16)
    ff = jnp.dot(a_ref[...], wd_ref[...], preferred_element_type=F32)
    out_ref[...] = out_ref[...] + _rms(ff, gfpost_ref[...])


def _out_ffn(ym2d, main2d, x2d, wo, gpost, gpre, wg, wu, wd, gfpost, dww, dwb, cnw, cnb, *, tm, tiles_per_seq):
    t, d = x2d.shape
    mw = ym2d.shape[1]
    n_u_slabs, dw_taps = dww.shape[0], dww.shape[1]
    cw = n_u_slabs * LANES
    d_mix = mw + cw
    d_ff = wg.shape[1]
    n = t // tm
    assert d_ff % MXU_N == 0 and main2d.shape[1] == 4 * mw + 2 * cw and (4 * mw) % (2 * cw) == 0
    u_pad = -(-(dw_taps - 1) // SUBLANES) * SUBLANES
    const = lambda i: (0, 0)
    const3 = lambda i: (0, 0, 0)
    ffn_tile = lambda i: (jnp.maximum(i - 1, 0), 0)
    conv_tile = lambda i: (jnp.minimum(i, n - 1), 4 * mw // (2 * cw))
    resident = functools.partial(pl.BlockSpec, index_map=const, pipeline_mode=pl.Buffered(1))
    vmem = (2 * tm * mw * 2 + 2 * tm * 2 * cw * 2 + 4 * tm * d * 4 + (d_mix * d + 3 * d * d_ff) * 2 + tm * d_ff * 2
            + (2 * (tm + u_pad) + tm) * cw * 4 + 2 * tm * cw * 2 + 8 * tm * d * 4)
    kern = functools.partial(_out_ffn_kernel, ff_chunk=MXU_N, dw_taps=dw_taps, tiles_per_seq=tiles_per_seq,
                             row_blk=64)
    return pl.pallas_call(
        kern,
        out_shape=jax.ShapeDtypeStruct((t, d), F32),
        grid=(n + 1,),
        in_specs=[
            pl.BlockSpec((tm, mw), ffn_tile),
            pl.BlockSpec((tm, 2 * cw), conv_tile),
            pl.BlockSpec((tm, d), ffn_tile),
            resident((d_mix, d)),
            pl.BlockSpec((1, d), const),
            pl.BlockSpec((1, d), const),
            resident((d, d_ff)),
            resident((d, d_ff)),
            resident((d_ff, d)),
            pl.BlockSpec((1, d), const),
            pl.BlockSpec(dww.shape, const3),
            pl.BlockSpec(dwb.shape, const3),
            pl.BlockSpec(cnw.shape, const3),
            pl.BlockSpec(cnb.shape, const3),
        ],
        out_specs=pl.BlockSpec((tm, d), ffn_tile),
        scratch_shapes=[
            pltpu.VMEM((tm, d_ff), BF16),
            pltpu.VMEM((2, n_u_slabs, tm + u_pad, LANES), F32),
            pltpu.VMEM((n_u_slabs, tm, LANES), F32),
            pltpu.VMEM((2, tm, cw), BF16),
        ],
        compiler_params=pltpu.CompilerParams(
            dimension_semantics=("arbitrary",), vmem_limit_bytes=min(vmem, VMEM_BYTES - (8 << 20))),
        name="out_ffn",
    )(ym2d, main2d, x2d, wo, gpost, gpre, wg, wu, wd, gfpost, dww, dwb, cnw, cnb)


def _pick_tile(n, target):
    t = min(n, target)
    while n % t:
        t //= 2
    return t


def _slabs(a):
    a = a.astype(F32).reshape(a.shape[0] if a.ndim == 2 else 1, -1, LANES)
    return jnp.transpose(a, (1, 0, 2))


def _layer(h, ln_mix_pre, ln_mix_post, w_in, qk_conv_w, qk_conv_b, i_bias, f_bias, mh_norm_w,
           dw_conv_w, dw_conv_b, conv_norm_w, conv_norm_b, w_out, ln_ffn_pre, ln_ffn_post,
           w_gate, w_up, w_down):
    b, s, d = h.shape
    n_heads = i_bias.shape[0]
    mw = mh_norm_w.shape[0]
    n_gate = 2 * n_heads
    row = lambda a: a.reshape(1, -1).astype(F32)

    w_main = jnp.concatenate([w_in[:, :4 * mw], w_in[:, 4 * mw + n_gate:]], axis=1).astype(BF16)
    w_gates = jnp.pad(w_in[:, 4 * mw:4 * mw + n_gate], ((0, 0), (0, LANES - n_gate))).astype(BF16)
    gbias = jnp.broadcast_to(
        jnp.pad(jnp.concatenate([i_bias, f_bias]), (0, SUBLANES - n_gate)).reshape(SUBLANES, 1).astype(F32),
        (SUBLANES, CHUNK))

    x2d = h.reshape(b * s, d)
    tm = _pick_tile(b * s, 512)
    main, gates, vt = _in_proj(x2d, row(ln_mix_pre), w_main, w_gates, tm=tm, mw=mw)

    ts = _pick_tile(s, 512)
    assert ts % CHUNK == 0
    ym = _mixer(main.reshape(b, s, -1), gates, vt, _slabs(qk_conv_w), _slabs(qk_conv_b), gbias,
                row(mh_norm_w), n_heads=n_heads, ts=ts)

    out = _out_ffn(ym.reshape(b * s, -1), main, x2d, w_out.astype(BF16), row(ln_mix_post), row(ln_ffn_pre),
                   w_gate.astype(BF16), w_up.astype(BF16), w_down.astype(BF16), row(ln_ffn_post),
                   _slabs(dw_conv_w), _slabs(dw_conv_b), _slabs(conv_norm_w), _slabs(conv_norm_b),
                   tm=ts, tiles_per_seq=s // ts)
    return out.reshape(b, s, d)


def kernel(x, ln_mix_pre, ln_mix_post, w_in, qk_conv_w, qk_conv_b, i_bias, f_bias, mh_norm_w,
           dw_conv_w, dw_conv_b, conv_norm_w, conv_norm_b, w_out, ln_ffn_pre, ln_ffn_post,
           w_gate, w_up, w_down):
    h = x
    for l in range(ln_mix_pre.shape[0]):
        h = _layer(h, ln_mix_pre[l], ln_mix_post[l], w_in[l], qk_conv_w[l], qk_conv_b[l], i_bias[l],
                   f_bias[l], mh_norm_w[l], dw_conv_w[l], dw_conv_b[l], conv_norm_w[l], conv_norm_b[l],
                   w_out[l], ln_ffn_pre[l], ln_ffn_post[l], w_gate[l], w_up[l], w_down[l])
    return h
```

```python
import functools

import jax
import jax.numpy as jnp
from jax import lax
from jax.experimental import pallas as pl
from jax.experimental.pallas import tpu as pltpu

F32 = jnp.float32
BF16 = jnp.bfloat16

NORM_EPS = 1e-6
LANES = 128
SUBLANES = 8
MXU_N = 256
VMEM_BYTES = 64 * 1024 * 1024
CHUNK = 128
CONV_ROW_STRIDE = 4
CONV_BLOCK_ROWS = 32


def _rms(x, w):
    return x * lax.rsqrt(jnp.mean(x * x, axis=-1, keepdims=True) + NORM_EPS) * w


def _sigmoid(x):
    return 1.0 / (1.0 + jnp.exp(-x))


def _log_sigmoid(x):
    return -(jnp.maximum(-x, 0.0) + jnp.log1p(jnp.exp(-jnp.abs(x))))


def _group_norm_lanes(x, w, b=None):
    mu = jnp.mean(x, axis=-1, keepdims=True)
    xc = x - mu
    var = jnp.mean(xc * xc, axis=-1, keepdims=True)
    y = xc * lax.rsqrt(var + NORM_EPS) * w
    return y if b is None else y + b


def _in_proj_kernel(x_ref, g_ref, wm_ref, wg_ref, qkw_ref, qkb_ref,
                    qkact_ref, vo_ref, u_ref, gates_ref, vt_ref,
                    qkext_ref, qkst_ref,
                    *, mw, cw, qk_taps, tiles_per_seq):
    i = pl.program_id(0)
    tm = x_ref.shape[0]
    P, n_blk, nc = CONV_ROW_STRIDE, CONV_BLOCK_ROWS, MXU_N
    qk_pad = qkext_ref.shape[2] - tm
    n_qk_slabs = 2 * mw // LANES
    slot = i % 2
    prev = 1 - slot

    @pl.when(i == 0)
    def _():
        qkext_ref[1] = jnp.zeros(qkext_ref.shape[1:], F32)

    xn = _rms(x_ref[...], g_ref[...]).astype(BF16)

    def proj(col0):
        return jnp.dot(xn, wm_ref[:, col0:col0 + nc], preferred_element_type=F32)

    seq_start = (jnp.minimum(i, pl.num_programs(0) - 2) % tiles_per_seq) == 0
    for g in range(n_qk_slabs):
        qkext_ref[slot, g, 0:qk_pad, :] = jnp.where(seq_start, 0.0, qkext_ref[prev, g, tm:tm + qk_pad, :])
    for c in range(2 * mw // nc):
        res = proj(c * nc)
        for j in range(nc // LANES):
            qkext_ref[slot, c * (nc // LANES) + j, qk_pad:qk_pad + tm, :] = res[:, j * LANES:(j + 1) * LANES]

    q_scale = LANES ** -0.5

    def qk_conv(g, ph, r0):
        acc = jnp.broadcast_to(qkb_ref[g], (n_blk, LANES))
        for k in range(qk_taps):
            start = qk_pad - (qk_taps - 1) + k + ph + r0
            acc = acc + qkext_ref[prev, g, pl.ds(start, n_blk, stride=P), :] * qkw_ref[g, k:k + 1, :]
        act = acc * _sigmoid(acc)
        if g < n_qk_slabs // 2:
            act = act * q_scale
        qkst_ref[g, pl.ds(ph + r0, n_blk, stride=P), :] = act
        return act

    conv_pieces = [(g, ph, r0) for g in range(n_qk_slabs) for ph in range(P) for r0 in range(0, tm, n_blk * P)]
    n_hosts = (2 * mw + cw) // nc
    per_host = -(-len(conv_pieces) // n_hosts)
    hosts = iter(range(n_hosts))
    ties = [None]

    def hosted(res):
        if ties[0] is not None:
            res = res + jnp.concatenate([ties[0]] * (res.shape[1] // LANES), axis=1)
        k = next(hosts)
        tie = jnp.zeros((1, LANES), F32)
        for piece in conv_pieces[k * per_host:(k + 1) * per_host]:
            tie = tie + jnp.where(qk_conv(*piece)[0:1, :] > jnp.inf, 1.0, 0.0)
        ties[0] = tie
        return res

    for c in range(mw // nc):
        res = hosted(proj(2 * mw + c * nc))
        vo_ref[:, c * nc:(c + 1) * nc] = res.astype(BF16)
        for j in range(vt_ref.shape[0]):
            vt_ref[j, c * nc:(c + 1) * nc, :] = res[j * CHUNK:(j + 1) * CHUNK, :].T.astype(BF16)
    for c in range(mw // nc):
        vo_ref[:, mw + c * nc:mw + (c + 1) * nc] = hosted(proj(3 * mw + c * nc)).astype(BF16)
    for c in range(cw // nc):
        glu = hosted(proj(4 * mw + c * nc) * _sigmoid(proj(4 * mw + cw + c * nc)))
        u_ref[:, c * nc:(c + 1) * nc] = glu.astype(BF16)
    gates = jnp.dot(xn, wg_ref[...], preferred_element_type=F32)
    gates = gates.T[0:gates_ref.shape[0], :]
    gates_ref[...] = gates + jnp.concatenate([ties[0]] * (tm // LANES), axis=1)
    for g in range(n_qk_slabs):
        qkact_ref[:, g * LANES:(g + 1) * LANES] = qkst_ref[g].astype(BF16)


def _in_proj(x2d, g, w_main, w_gate, qkw, qkb, *, tm, mw, tiles_per_seq):
    t, d = x2d.shape
    n_main = w_main.shape[1]
    cw = (n_main - 4 * mw) // 2
    qk_taps = qkw.shape[1]
    qk_pad = -(-(qk_taps - 1) // SUBLANES) * SUBLANES
    n = t // tm
    assert qkw.shape[0] * LANES == 2 * mw and n_main == 4 * mw + 2 * cw
    const2 = lambda i: (0, 0)
    const3 = lambda i: (0, 0, 0)
    cur2 = lambda i: (jnp.minimum(i, n - 1), 0)
    prv2 = lambda i: (jnp.maximum(i - 1, 0), 0)
    resident = functools.partial(pl.BlockSpec, index_map=const2, pipeline_mode=pl.Buffered(1))
    vmem = (2 * tm * d * 4 + (d * n_main + d * LANES) * 2 + 2 * tm * (2 * mw + 2 * mw + cw + mw) * 2
            + 2 * tm * SUBLANES * 4 + (2 * (tm + qk_pad) + tm) * 2 * mw * 4 + 8 * tm * d * 4)
    kern = functools.partial(_in_proj_kernel, mw=mw, cw=cw, qk_taps=qk_taps, tiles_per_seq=tiles_per_seq)
    return pl.pallas_call(
        kern,
        out_shape=(jax.ShapeDtypeStruct((t, 2 * mw), BF16),
                   jax.ShapeDtypeStruct((t, 2 * mw), BF16),
                   jax.ShapeDtypeStruct((t, cw), BF16),
                   jax.ShapeDtypeStruct((SUBLANES, t), F32),
                   jax.ShapeDtypeStruct((t // CHUNK, mw, CHUNK), BF16)),
        grid=(n + 1,),
        in_specs=[
            pl.BlockSpec((tm, d), cur2),
            pl.BlockSpec((1, d), const2),
            resident((d, n_main)),
            resident((d, LANES)),
            pl.BlockSpec(qkw.shape, const3),
            pl.BlockSpec(qkb.shape, const3),
        ],
        out_specs=(pl.BlockSpec((tm, 2 * mw), prv2),
                   pl.BlockSpec((tm, 2 * mw), cur2),
                   pl.BlockSpec((tm, cw), cur2),
                   pl.BlockSpec((SUBLANES, tm), lambda i: (0, jnp.minimum(i, n - 1))),
                   pl.BlockSpec((tm // CHUNK, mw, CHUNK), lambda i: (jnp.minimum(i, n - 1), 0, 0))),
        scratch_shapes=[
            pltpu.VMEM((2, 2 * mw // LANES, tm + qk_pad, LANES), F32),
            pltpu.VMEM((2 * mw // LANES, tm, LANES), F32),
        ],
        compiler_params=pltpu.CompilerParams(
            dimension_semantics=("arbitrary",), vmem_limit_bytes=min(vmem, VMEM_BYTES - (8 << 20))),
        name="in_proj",
    )(x2d, g, w_main, w_gate, qkw, qkb)


def _prefix_max_rows(x):
    sub = lax.broadcasted_iota(jnp.int32, (SUBLANES, x.shape[1]), 0)
    out, run = [], None
    for j in range(x.shape[0] // SUBLANES):
        blk = x[j * SUBLANES:(j + 1) * SUBLANES, :]
        shift = 1
        while shift < SUBLANES:
            blk = jnp.maximum(blk, jnp.where(sub >= shift, pltpu.roll(blk, shift, 0), -jnp.inf))
            shift *= 2
        if run is not None:
            blk = jnp.maximum(blk, run)
        run = jnp.broadcast_to(blk[SUBLANES - 1:SUBLANES, :], blk.shape)
        out.append(blk)
    return jnp.concatenate(out, axis=0)


def _mixer_kernel(qk_ref, v_ref, o_ref, gates_ref, vt_ref, gbias_ref, mhw_ref,
                  y_ref,
                  ga_ref, gb_ref, gcm_ref, gbt_ref, c_ref, m_ref, mlane_ref,
                  *, n_heads):
    ts = qk_ref.shape[0]
    mw = v_ref.shape[1]
    hd = mw // n_heads
    L = CHUNK

    @pl.when(pl.program_id(1) == 0)
    def _():
        c_ref[...] = jnp.zeros(c_ref.shape, F32)
        m_ref[...] = jnp.zeros(m_ref.shape, F32)
        mlane_ref[...] = jnp.zeros(mlane_ref.shape, F32)

    row_i = lax.broadcasted_iota(jnp.int32, (L, L), 0)
    col_i = lax.broadcasted_iota(jnp.int32, (L, L), 1)
    causal = row_i >= col_i
    triu = jnp.where(row_i <= col_i, 1.0, 0.0).astype(BF16)
    head_row = lax.broadcasted_iota(jnp.int32, (SUBLANES, L), 0) < n_heads
    pad_rows = jnp.zeros((LANES - SUBLANES, L), F32)
    for c in range(ts // L):
        g_pre = gates_ref[:, c * L:(c + 1) * L] + gbias_ref[...]
        logf = _log_sigmoid(g_pre)
        hi = logf.astype(BF16)
        r1 = logf - hi.astype(F32)
        mid = r1.astype(BF16)
        lo = (r1 - mid.astype(F32)).astype(BF16)
        b_all = (jnp.dot(lo, triu, preferred_element_type=F32) + jnp.dot(mid, triu, preferred_element_type=F32)
                 + jnp.dot(hi, triu, preferred_element_type=F32))
        b = jnp.where(head_row, pltpu.roll(b_all, SUBLANES - n_heads, 0), 0.0)
        a = jnp.where(head_row, g_pre - b, 0.0)
        ga_ref[c] = a
        gb_ref[c] = b
        gcm_ref[c] = _prefix_max_rows(jnp.concatenate([a, pad_rows], axis=0).T)
        gbt_ref[c] = jnp.concatenate([b, pad_rows], axis=0).T

    ones_blk = jnp.ones((L, hd), BF16)
    nt = (((1,), (1,)), ((), ()))

    def chunk_step(c, carry):
        rows = pl.ds(pl.multiple_of(c * L, L), L)
        a, b = ga_ref[c], gb_ref[c]
        m_prev = m_ref[...]
        m_end = jnp.maximum(m_prev, jnp.max(a, axis=1, keepdims=True))
        decay = jnp.exp(m_prev - m_end)
        wa = jnp.exp(a - m_end)
        m_ref[...] = b[:, L - 1:L] + m_end
        mm_all = jnp.maximum(mlane_ref[...], gcm_ref[c])
        m_all = gbt_ref[c] + mm_all
        mlane_ref[...] = m_all[L - 1:L, :]

        heads = range(n_heads)
        hls = [slice(h * hd, (h + 1) * hd) for h in heads]
        qc = [qk_ref[rows, hls[h]] for h in heads]
        kc = [qk_ref[rows, mw + h * hd:mw + (h + 1) * hd] for h in heads]
        v_aug = [jnp.concatenate([v_ref[rows, hls[h]], ones_blk], axis=1) for h in heads]
        ct = [c_ref[h] for h in heads]
        qk = [lax.dot_general(qc[h], kc[h], nt, preferred_element_type=F32) for h in heads]
        inter = [lax.dot_general(qc[h], ct[h].astype(BF16), nt, preferred_element_type=F32) for h in heads]
        for h in heads:
            wa_h = wa[h:h + 1, :]
            vtw = jnp.concatenate(
                [vt_ref[c, hls[h], :].astype(F32) * wa_h, jnp.broadcast_to(wa_h, (hd, L))], axis=0)
            c_ref[h] = decay[h:h + 1, :] * ct[h] + jnp.dot(vtw.astype(BF16), kc[h], preferred_element_type=F32)

        mm_col = [mm_all[:, h:h + 1] for h in heads]
        s = [(qk[h] * jnp.where(causal, jnp.exp(a[h:h + 1, :] - mm_col[h]), 0.0)).astype(BF16)
             for h in heads]
        intra = [jnp.dot(s[h], v_aug[h], preferred_element_type=F32) for h in heads]
        for h in heads:
            w_inter = jnp.exp(m_prev[h:h + 1, :] - mm_col[h])
            em = jnp.exp(-jnp.broadcast_to(m_all[:, h:h + 1], (L, LANES)))
            r = jnp.concatenate([w_inter, w_inter], axis=1) * inter[h] + intra[h]
            cell = r[:, :hd] * (1.0 / jnp.maximum(jnp.abs(r[:, hd:]), em))
            yn = _group_norm_lanes(cell, mhw_ref[:, hls[h]])
            y_ref[rows, hls[h]] = (_sigmoid(o_ref[rows, hls[h]].astype(F32)) * yn).astype(BF16)
        return carry

    lax.fori_loop(0, ts // L, chunk_step, 0, unroll=2)


def _mixer(qkact, vo, gates, vt, gbias, mhw, *, n_heads, ts):
    b, s, _ = qkact.shape
    mw = mhw.shape[1]
    hd = mw // n_heads
    assert hd == LANES and CHUNK == hd and n_heads <= SUBLANES // 2
    const2 = lambda i, j: (0, 0)
    kern = functools.partial(_mixer_kernel, n_heads=n_heads)
    vmem = (2 * ts * (2 * mw + 2 * mw + mw) * 2 + 2 * ts * SUBLANES * 4 + 2 * ts * mw * 2
            + n_heads * hd * 2 * hd * 4 + (16 << 20))
    return pl.pallas_call(
        kern,
        out_shape=jax.ShapeDtypeStruct((b, s, mw), BF16),
        grid=(b, s // ts),
        in_specs=[
            pl.BlockSpec((None, ts, 2 * mw), lambda i, j: (i, j, 0)),
            pl.BlockSpec((None, ts, mw), lambda i, j: (i, j, 0)),
            pl.BlockSpec((None, ts, mw), lambda i, j: (i, j, 1)),
            pl.BlockSpec((SUBLANES, ts), lambda i, j: (0, i * (s // ts) + j)),
            pl.BlockSpec((ts // CHUNK, mw, CHUNK), lambda i, j: (i * (s // ts) + j, 0, 0)),
            pl.BlockSpec((SUBLANES, CHUNK), const2),
            pl.BlockSpec((1, mw), const2),
        ],
        out_specs=pl.BlockSpec((None, ts, mw), lambda i, j: (i, j, 0)),
        scratch_shapes=[
            pltpu.VMEM((ts // CHUNK, SUBLANES, CHUNK), F32),
            pltpu.VMEM((ts // CHUNK, SUBLANES, CHUNK), F32),
            pltpu.VMEM((ts // CHUNK, CHUNK, LANES), F32),
            pltpu.VMEM((ts // CHUNK, CHUNK, LANES), F32),
            pltpu.VMEM((n_heads, 2 * hd, hd), F32),
            pltpu.VMEM((SUBLANES, CHUNK), F32),
            pltpu.VMEM((1, LANES), F32),
        ],
        compiler_params=pltpu.CompilerParams(
            dimension_semantics=("arbitrary", "arbitrary"), vmem_limit_bytes=min(vmem, VMEM_BYTES - (8 << 20))),
        name="mixer",
    )(qkact, vo, vo, gates, vt, gbias, mhw)


def _out_ffn_kernel(ym_ref, u_ref, x_ref, wo_ref, gpost_ref, gpre_ref, wg_ref, wu_ref, wd_ref, gfpost_ref,
                    dww_ref, dwb_ref, cnw_ref, cnb_ref,
                    out_ref, a_ref, uext_ref, yst_ref, yc_ref,
                    *, ff_chunk, dw_taps, tiles_per_seq, row_blk):
    i = pl.program_id(0)
    tm = x_ref.shape[0]
    mw = ym_ref.shape[1]
    cw = u_ref.shape[1]
    P, n_blk = CONV_ROW_STRIDE, CONV_BLOCK_ROWS
    u_pad = uext_ref.shape[2] - tm
    n_u_slabs = cw // LANES
    slot = i % 2
    prev = 1 - slot

    @pl.when(i == 0)
    def _():
        uext_ref[1] = jnp.zeros(uext_ref.shape[1:], F32)
        yc_ref[1] = jnp.zeros(yc_ref.shape[1:], BF16)

    seq_start = (jnp.minimum(i, pl.num_programs(0) - 2) % tiles_per_seq) == 0
    for g in range(n_u_slabs):
        uext_ref[slot, g, 0:u_pad, :] = jnp.where(seq_start, 0.0, uext_ref[prev, g, tm:tm + u_pad, :])
    for blk in range(tm // row_blk):
        rows = slice(blk * row_blk, (blk + 1) * row_blk)
        for g in range(n_u_slabs):
            uext_ref[slot, g, u_pad + rows.start:u_pad + rows.stop, :] = (
                u_ref[rows, g * LANES:(g + 1) * LANES].astype(F32))

    def dw_conv(g, ph, r0):
        acc = jnp.broadcast_to(dwb_ref[g], (n_blk, LANES))
        for k in range(dw_taps):
            start = u_pad - (dw_taps - 1) + k + ph + r0
            acc = acc + uext_ref[slot, g, pl.ds(start, n_blk, stride=P), :] * dww_ref[g, k:k + 1, :]
        mu = jnp.mean(acc, axis=-1, keepdims=True)
        var = jnp.maximum(jnp.mean(acc * acc, axis=-1, keepdims=True) - mu * mu, 0.0)
        yn = (acc - mu) * lax.rsqrt(var + NORM_EPS) * cnw_ref[g] + cnb_ref[g]
        y = yn * _sigmoid(yn)
        yst_ref[g, pl.ds(ph + r0, n_blk, stride=P), :] = y
        return y

    conv_pieces = [(g, ph, r0) for g in range(n_u_slabs) for ph in range(P) for r0 in range(0, tm, n_blk * P)]

    mix = (jnp.dot(ym_ref[...], wo_ref[0:mw, :], preferred_element_type=F32)
           + jnp.dot(yc_ref[prev], wo_ref[mw:, :], preferred_element_type=F32))
    out_ref[...] = x_ref[...] + _rms(mix, gpost_ref[...])
    hn = _rms(out_ref[...], gpre_ref[...]).astype(BF16)
    d_ff = wg_ref.shape[1]
    n_chunks = d_ff // ff_chunk
    per_chunk = -(-len(conv_pieces) // n_chunks)
    tie = None
    for c in range(n_chunks):
        sl = slice(c * ff_chunk, (c + 1) * ff_chunk)
        g = jnp.dot(hn, wg_ref[:, sl], preferred_element_type=F32)
        u = jnp.dot(hn, wu_ref[:, sl], preferred_element_type=F32)
        act = g * _sigmoid(g) * u
        if tie is not None:
            act = act + tie
        a_ref[:, sl] = act.astype(BF16)
        tie = None
        for piece in conv_pieces[c * per_chunk:(c + 1) * per_chunk]:
            z = jnp.where(dw_conv(*piece)[0:1, :] > jnp.inf, 1.0, 0.0)
            tie = z if tie is None else tie + z
        if tie is not None:
            tie = jnp.concatenate([tie] * (ff_chunk // LANES), axis=1)
    for g in range(n_u_slabs):
        yc_ref[slot, :, g * LANES:(g + 1) * LANES] = yst_ref[g].astype(BF16)
    ff = jnp.dot(a_ref[...], wd_ref[...], preferred_element_type=F32)
    out_ref[...] = out_ref[...] + _rms(ff, gfpost_ref[...])


def _out_ffn(ym2d, u2d, x2d, wo, gpost, gpre, wg, wu, wd, gfpost, dww, dwb, cnw, cnb, *, tm, tiles_per_seq):
    t, d = x2d.shape
    mw = ym2d.shape[1]
    n_u_slabs, dw_taps = dww.shape[0], dww.shape[1]
    cw = n_u_slabs * LANES
    d_mix = mw + cw
    d_ff = wg.shape[1]
    n = t // tm
    assert d_ff % MXU_N == 0 and u2d.shape[1] == cw
    u_pad = -(-(dw_taps - 1) // SUBLANES) * SUBLANES
    const = lambda i: (0, 0)
    const3 = lambda i: (0, 0, 0)
    ffn_tile = lambda i: (jnp.maximum(i - 1, 0), 0)
    conv_tile = lambda i: (jnp.minimum(i, n - 1), 0)
    resident = functools.partial(pl.BlockSpec, index_map=const, pipeline_mode=pl.Buffered(1))
    vmem = (2 * tm * mw * 2 + 2 * tm * cw * 2 + 4 * tm * d * 4 + (d_mix * d + 3 * d * d_ff) * 2 + tm * d_ff * 2
            + (2 * (tm + u_pad) + tm) * cw * 4 + 2 * tm * cw * 2 + 8 * tm * d * 4)
    kern = functools.partial(_out_ffn_kernel, ff_chunk=MXU_N, dw_taps=dw_taps, tiles_per_seq=tiles_per_seq,
                             row_blk=64)
    return pl.pallas_call(
        kern,
        out_shape=jax.ShapeDtypeStruct((t, d), F32),
        grid=(n + 1,),
        in_specs=[
            pl.BlockSpec((tm, mw), ffn_tile),
            pl.BlockSpec((tm, cw), conv_tile),
            pl.BlockSpec((tm, d), ffn_tile),
            resident((d_mix, d)),
            pl.BlockSpec((1, d), const),
            pl.BlockSpec((1, d), const),
            resident((d, d_ff)),
            resident((d, d_ff)),
            resident((d_ff, d)),
            pl.BlockSpec((1, d), const),
            pl.BlockSpec(dww.shape, const3),
            pl.BlockSpec(dwb.shape, const3),
            pl.BlockSpec(cnw.shape, const3),
            pl.BlockSpec(cnb.shape, const3),
        ],
        out_specs=pl.BlockSpec((tm, d), ffn_tile),
        scratch_shapes=[
            pltpu.VMEM((tm, d_ff), BF16),
            pltpu.VMEM((2, n_u_slabs, tm + u_pad, LANES), F32),
            pltpu.VMEM((n_u_slabs, tm, LANES), F32),
            pltpu.VMEM((2, tm, cw), BF16),
        ],
        compiler_params=pltpu.CompilerParams(
            dimension_semantics=("arbitrary",), vmem_limit_bytes=min(vmem, VMEM_BYTES - (8 << 20))),
        name="out_ffn",
    )(ym2d, u2d, x2d, wo, gpost, gpre, wg, wu, wd, gfpost, dww, dwb, cnw, cnb)


def _pick_tile(n, target):
    t = min(n, target)
    while n % t:
        t //= 2
    return t


def _slabs(a):
    a = a.astype(F32).reshape(a.shape[0] if a.ndim == 2 else 1, -1, LANES)
    return jnp.transpose(a, (1, 0, 2))


def _layer(h, ln_mix_pre, ln_mix_post, w_in, qk_conv_w, qk_conv_b, i_bias, f_bias, mh_norm_w,
           dw_conv_w, dw_conv_b, conv_norm_w, conv_norm_b, w_out, ln_ffn_pre, ln_ffn_post,
           w_gate, w_up, w_down):
    b, s, d = h.shape
    n_heads = i_bias.shape[0]
    mw = mh_norm_w.shape[0]
    n_gate = 2 * n_heads
    row = lambda a: a.reshape(1, -1).astype(F32)

    w_main = jnp.concatenate([w_in[:, :4 * mw], w_in[:, 4 * mw + n_gate:]], axis=1).astype(BF16)
    w_gates = jnp.pad(w_in[:, 4 * mw:4 * mw + n_gate], ((0, 0), (0, LANES - n_gate))).astype(BF16)
    gbias = jnp.broadcast_to(
        jnp.pad(jnp.concatenate([i_bias, f_bias]), (0, SUBLANES - n_gate)).reshape(SUBLANES, 1).astype(F32),
        (SUBLANES, CHUNK))

    x2d = h.reshape(b * s, d)
    ts = _pick_tile(s, 512)
    assert ts % CHUNK == 0
    qkact, vo, u, gates, vt = _in_proj(x2d, row(ln_mix_pre), w_main, w_gates, _slabs(qk_conv_w),
                                       _slabs(qk_conv_b), tm=ts, mw=mw, tiles_per_seq=s // ts)

    ym = _mixer(qkact.reshape(b, s, -1), vo.reshape(b, s, -1), gates, vt, gbias, row(mh_norm_w),
                n_heads=n_heads, ts=ts)

    out = _out_ffn(ym.reshape(b * s, -1), u, x2d, w_out.astype(BF16), row(ln_mix_post), row(ln_ffn_pre),
                   w_gate.astype(BF16), w_up.astype(BF16), w_down.astype(BF16), row(ln_ffn_post),
                   _slabs(dw_conv_w), _slabs(dw_conv_b), _slabs(conv_norm_w), _slabs(conv_norm_b),
                   tm=ts, tiles_per_seq=s // ts)
    return out.reshape(b, s, d)


def kernel(x, ln_mix_pre, ln_mix_post, w_in, qk_conv_w, qk_conv_b, i_bias, f_bias, mh_norm_w,
           dw_conv_w, dw_conv_b, conv_norm_w, conv_norm_b, w_out, ln_ffn_pre, ln_ffn_post,
           w_gate, w_up, w_down):
    h = x
    for l in range(ln_mix_pre.shape[0]):
        h = _layer(h, ln_mix_pre[l], ln_mix_post[l], w_in[l], qk_conv_w[l], qk_conv_b[l], i_bias[l],
                   f_bias[l], mh_norm_w[l], dw_conv_w[l], dw_conv_b[l], conv_norm_w[l], conv_norm_b[l],
                   w_out[l], ln_ffn_pre[l], ln_ffn_post[l], w_gate[l], w_up[l], w_down[l])
    return h
```

```python
import functools

import jax
import jax.numpy as jnp
from jax import lax
from jax.experimental import pallas as pl
from jax.experimental.pallas import tpu as pltpu

F32 = jnp.float32
BF16 = jnp.bfloat16

NORM_EPS = 1e-6
LANES = 128
SUBLANES = 8
MXU_N = 256
VMEM_BYTES = 64 * 1024 * 1024
CHUNK = 128
CONV_ROW_STRIDE = 4
CONV_BLOCK_ROWS = 32


def _rms(x, w):
    return x * lax.rsqrt(jnp.mean(x * x, axis=-1, keepdims=True) + NORM_EPS) * w


def _sigmoid(x):
    return 1.0 / (1.0 + jnp.exp(-x))


def _log_sigmoid(x):
    return -(jnp.maximum(-x, 0.0) + jnp.log1p(jnp.exp(-jnp.abs(x))))


def _group_norm_lanes(x, w, b=None):
    mu = jnp.mean(x, axis=-1, keepdims=True)
    xc = x - mu
    var = jnp.mean(xc * xc, axis=-1, keepdims=True)
    y = xc * lax.rsqrt(var + NORM_EPS) * w
    return y if b is None else y + b


def _in_proj_kernel(x_ref, g_ref, wm_ref, wg_ref, qkw_ref, qkb_ref,
                    qkact_ref, vo_ref, u_ref, gates_ref, vt_ref,
                    qkext_ref, qkst_ref,
                    *, mw, cw, qk_taps, tiles_per_seq):
    i = pl.program_id(0)
    tm = x_ref.shape[0]
    P, n_blk, nc = CONV_ROW_STRIDE, CONV_BLOCK_ROWS, MXU_N
    qk_pad = qkext_ref.shape[2] - tm
    n_qk_slabs = 2 * mw // LANES
    slot = i % 2
    prev = 1 - slot

    @pl.when(i == 0)
    def _():
        qkext_ref[1] = jnp.zeros(qkext_ref.shape[1:], F32)

    xn = _rms(x_ref[...], g_ref[...]).astype(BF16)

    def proj(col0):
        return jnp.dot(xn, wm_ref[:, col0:col0 + nc], preferred_element_type=F32)

    seq_start = (jnp.minimum(i, pl.num_programs(0) - 2) % tiles_per_seq) == 0
    for g in range(n_qk_slabs):
        qkext_ref[slot, g, 0:qk_pad, :] = jnp.where(seq_start, 0.0, qkext_ref[prev, g, tm:tm + qk_pad, :])
    for c in range(2 * mw // nc):
        res = proj(c * nc)
        for j in range(nc // LANES):
            qkext_ref[slot, c * (nc // LANES) + j, qk_pad:qk_pad + tm, :] = res[:, j * LANES:(j + 1) * LANES]

    q_scale = LANES ** -0.5

    def qk_conv(g, ph, r0, after):
        acc = jnp.broadcast_to(qkb_ref[g] + after, (n_blk, LANES))
        for k in range(qk_taps):
            start = qk_pad - (qk_taps - 1) + k + ph + r0
            acc = acc + qkext_ref[prev, g, pl.ds(start, n_blk, stride=P), :] * qkw_ref[g, k:k + 1, :]
        act = acc * _sigmoid(acc)
        if g < n_qk_slabs // 2:
            act = act * q_scale
        qkst_ref[g, pl.ds(ph + r0, n_blk, stride=P), :] = act
        return act

    conv_pieces = [(g, ph, r0) for g in range(n_qk_slabs) for ph in range(P) for r0 in range(0, tm, n_blk * P)]
    n_hosts = (2 * mw + cw) // nc
    per_host = -(-len(conv_pieces) // n_hosts)
    hosts = iter(range(n_hosts))
    ties = [None]

    def hosted(res):
        after = jnp.where(res[tm - 1:tm, 0:LANES] > jnp.inf, 1.0, 0.0)
        if ties[0] is not None:
            res = res + jnp.concatenate([ties[0]] * (res.shape[1] // LANES), axis=1)
        k = next(hosts)
        tie = jnp.zeros((1, LANES), F32)
        for piece in conv_pieces[k * per_host:(k + 1) * per_host]:
            tie = tie + jnp.where(qk_conv(*piece, after)[0:1, :] > jnp.inf, 1.0, 0.0)
        ties[0] = tie
        return res

    for c in range(mw // nc):
        res = hosted(proj(2 * mw + c * nc))
        vo_ref[:, c * nc:(c + 1) * nc] = res.astype(BF16)
        for j in range(vt_ref.shape[0]):
            vt_ref[j, c * nc:(c + 1) * nc, :] = res[j * CHUNK:(j + 1) * CHUNK, :].T.astype(BF16)
    for c in range(mw // nc):
        vo_ref[:, mw + c * nc:mw + (c + 1) * nc] = hosted(proj(3 * mw + c * nc)).astype(BF16)
    for c in range(cw // nc):
        glu = hosted(proj(4 * mw + c * nc) * _sigmoid(proj(4 * mw + cw + c * nc)))
        u_ref[:, c * nc:(c + 1) * nc] = glu.astype(BF16)
    gates = jnp.dot(xn, wg_ref[...], preferred_element_type=F32)
    gates = gates.T[0:gates_ref.shape[0], :]
    gates_ref[...] = gates + jnp.concatenate([ties[0]] * (tm // LANES), axis=1)
    for g in range(n_qk_slabs):
        qkact_ref[:, g * LANES:(g + 1) * LANES] = qkst_ref[g].astype(BF16)


def _in_proj(x2d, g, w_main, w_gate, qkw, qkb, *, tm, mw, tiles_per_seq):
    t, d = x2d.shape
    n_main = w_main.shape[1]
    cw = (n_main - 4 * mw) // 2
    qk_taps = qkw.shape[1]
    qk_pad = -(-(qk_taps - 1) // SUBLANES) * SUBLANES
    n = t // tm
    assert qkw.shape[0] * LANES == 2 * mw and n_main == 4 * mw + 2 * cw
    const2 = lambda i: (0, 0)
    const3 = lambda i: (0, 0, 0)
    cur2 = lambda i: (jnp.minimum(i, n - 1), 0)
    prv2 = lambda i: (jnp.maximum(i - 1, 0), 0)
    resident = functools.partial(pl.BlockSpec, index_map=const2, pipeline_mode=pl.Buffered(1))
    vmem = (2 * tm * d * 4 + (d * n_main + d * LANES) * 2 + 2 * tm * (2 * mw + 2 * mw + cw + mw) * 2
            + 2 * tm * SUBLANES * 4 + (2 * (tm + qk_pad) + tm) * 2 * mw * 4 + 8 * tm * d * 4)
    kern = functools.partial(_in_proj_kernel, mw=mw, cw=cw, qk_taps=qk_taps, tiles_per_seq=tiles_per_seq)
    return pl.pallas_call(
        kern,
        out_shape=(jax.ShapeDtypeStruct((t, 2 * mw), BF16),
                   jax.ShapeDtypeStruct((t, 2 * mw), BF16),
                   jax.ShapeDtypeStruct((t, cw), BF16),
                   jax.ShapeDtypeStruct((SUBLANES, t), F32),
                   jax.ShapeDtypeStruct((t // CHUNK, mw, CHUNK), BF16)),
        grid=(n + 1,),
        in_specs=[
            pl.BlockSpec((tm, d), cur2),
            pl.BlockSpec((1, d), const2),
            resident((d, n_main)),
            resident((d, LANES)),
            pl.BlockSpec(qkw.shape, const3),
            pl.BlockSpec(qkb.shape, const3),
        ],
        out_specs=(pl.BlockSpec((tm, 2 * mw), prv2),
                   pl.BlockSpec((tm, 2 * mw), cur2),
                   pl.BlockSpec((tm, cw), cur2),
                   pl.BlockSpec((SUBLANES, tm), lambda i: (0, jnp.minimum(i, n - 1))),
                   pl.BlockSpec((tm // CHUNK, mw, CHUNK), lambda i: (jnp.minimum(i, n - 1), 0, 0))),
        scratch_shapes=[
            pltpu.VMEM((2, 2 * mw // LANES, tm + qk_pad, LANES), F32),
            pltpu.VMEM((2 * mw // LANES, tm, LANES), F32),
        ],
        compiler_params=pltpu.CompilerParams(
            dimension_semantics=("arbitrary",), vmem_limit_bytes=min(vmem, VMEM_BYTES - (8 << 20))),
        name="in_proj",
    )(x2d, g, w_main, w_gate, qkw, qkb)


def _prefix_max_rows(x):
    sub = lax.broadcasted_iota(jnp.int32, (SUBLANES, x.shape[1]), 0)
    out, run = [], None
    for j in range(x.shape[0] // SUBLANES):
        blk = x[j * SUBLANES:(j + 1) * SUBLANES, :]
        shift = 1
        while shift < SUBLANES:
            blk = jnp.maximum(blk, jnp.where(sub >= shift, pltpu.roll(blk, shift, 0), -jnp.inf))
            shift *= 2
        if run is not None:
            blk = jnp.maximum(blk, run)
        run = jnp.broadcast_to(blk[SUBLANES - 1:SUBLANES, :], blk.shape)
        out.append(blk)
    return jnp.concatenate(out, axis=0)


def _mixer_kernel(qk_ref, v_ref, o_ref, gates_ref, vt_ref, gbias_ref, mhw_ref,
                  y_ref,
                  ga_ref, gb_ref, gcm_ref, gbt_ref, c_ref, m_ref, mlane_ref,
                  *, n_heads):
    ts = qk_ref.shape[0]
    mw = v_ref.shape[1]
    hd = mw // n_heads
    L = CHUNK

    @pl.when(pl.program_id(1) == 0)
    def _():
        c_ref[...] = jnp.zeros(c_ref.shape, F32)
        m_ref[...] = jnp.zeros(m_ref.shape, F32)
        mlane_ref[...] = jnp.zeros(mlane_ref.shape, F32)

    row_i = lax.broadcasted_iota(jnp.int32, (L, L), 0)
    col_i = lax.broadcasted_iota(jnp.int32, (L, L), 1)
    causal = row_i >= col_i
    triu = jnp.where(row_i <= col_i, 1.0, 0.0).astype(BF16)
    head_row = lax.broadcasted_iota(jnp.int32, (SUBLANES, L), 0) < n_heads
    pad_rows = jnp.zeros((LANES - SUBLANES, L), F32)
    for c in range(ts // L):
        g_pre = gates_ref[:, c * L:(c + 1) * L] + gbias_ref[...]
        logf = _log_sigmoid(g_pre)
        hi = logf.astype(BF16)
        r1 = logf - hi.astype(F32)
        mid = r1.astype(BF16)
        lo = (r1 - mid.astype(F32)).astype(BF16)
        b_all = (jnp.dot(lo, triu, preferred_element_type=F32) + jnp.dot(mid, triu, preferred_element_type=F32)
                 + jnp.dot(hi, triu, preferred_element_type=F32))
        b = jnp.where(head_row, pltpu.roll(b_all, SUBLANES - n_heads, 0), 0.0)
        a = jnp.where(head_row, g_pre - b, 0.0)
        ga_ref[c] = a
        gb_ref[c] = b
        gcm_ref[c] = _prefix_max_rows(jnp.concatenate([a, pad_rows], axis=0).T)
        gbt_ref[c] = jnp.concatenate([b, pad_rows], axis=0).T

    ones_blk = jnp.ones((L, hd), BF16)
    nt = (((1,), (1,)), ((), ()))

    def chunk_step(c, carry):
        rows = pl.ds(pl.multiple_of(c * L, L), L)
        a, b = ga_ref[c], gb_ref[c]
        m_prev = m_ref[...]
        m_end = jnp.maximum(m_prev, jnp.max(a, axis=1, keepdims=True))
        decay = jnp.exp(m_prev - m_end)
        wa = jnp.exp(a - m_end)
        m_ref[...] = b[:, L - 1:L] + m_end
        mm_all = jnp.maximum(mlane_ref[...], gcm_ref[c])
        m_all = gbt_ref[c] + mm_all
        mlane_ref[...] = m_all[L - 1:L, :]

        heads = range(n_heads)
        hls = [slice(h * hd, (h + 1) * hd) for h in heads]
        qc = [qk_ref[rows, hls[h]] for h in heads]
        kc = [qk_ref[rows, mw + h * hd:mw + (h + 1) * hd] for h in heads]
        v_aug = [jnp.concatenate([v_ref[rows, hls[h]], ones_blk], axis=1) for h in heads]
        ct = [c_ref[h] for h in heads]
        qk = [lax.dot_general(qc[h], kc[h], nt, preferred_element_type=F32) for h in heads]
        inter = [lax.dot_general(qc[h], ct[h].astype(BF16), nt, preferred_element_type=F32) for h in heads]
        for h in heads:
            wa_h = wa[h:h + 1, :]
            vtw = jnp.concatenate(
                [vt_ref[c, hls[h], :].astype(F32) * wa_h, jnp.broadcast_to(wa_h, (hd, L))], axis=0)
            c_ref[h] = decay[h:h + 1, :] * ct[h] + jnp.dot(vtw.astype(BF16), kc[h], preferred_element_type=F32)

        mm_col = [mm_all[:, h:h + 1] for h in heads]
        s = [(qk[h] * jnp.where(causal, jnp.exp(a[h:h + 1, :] - mm_col[h]), 0.0)).astype(BF16)
             for h in heads]
        intra = [jnp.dot(s[h], v_aug[h], preferred_element_type=F32) for h in heads]
        for h in heads:
            w_inter = jnp.exp(m_prev[h:h + 1, :] - mm_col[h])
            em = jnp.exp(-jnp.broadcast_to(m_all[:, h:h + 1], (L, LANES)))
            r = jnp.concatenate([w_inter, w_inter], axis=1) * inter[h] + intra[h]
            cell = r[:, :hd] * (1.0 / jnp.maximum(jnp.abs(r[:, hd:]), em))
            yn = _group_norm_lanes(cell, mhw_ref[:, hls[h]])
            y_ref[rows, hls[h]] = (_sigmoid(o_ref[rows, hls[h]].astype(F32)) * yn).astype(BF16)
        return carry

    lax.fori_loop(0, ts // L, chunk_step, 0, unroll=2)


def _mixer(qkact, vo, gates, vt, gbias, mhw, *, n_heads, ts):
    b, s, _ = qkact.shape
    mw = mhw.shape[1]
    hd = mw // n_heads
    assert hd == LANES and CHUNK == hd and n_heads <= SUBLANES // 2
    const2 = lambda i, j: (0, 0)
    kern = functools.partial(_mixer_kernel, n_heads=n_heads)
    vmem = (2 * ts * (2 * mw + 2 * mw + mw) * 2 + 2 * ts * SUBLANES * 4 + 2 * ts * mw * 2
            + n_heads * hd * 2 * hd * 4 + (16 << 20))
    return pl.pallas_call(
        kern,
        out_shape=jax.ShapeDtypeStruct((b, s, mw), BF16),
        grid=(b, s // ts),
        in_specs=[
            pl.BlockSpec((None, ts, 2 * mw), lambda i, j: (i, j, 0)),
            pl.BlockSpec((None, ts, mw), lambda i, j: (i, j, 0)),
            pl.BlockSpec((None, ts, mw), lambda i, j: (i, j, 1)),
            pl.BlockSpec((SUBLANES, ts), lambda i, j: (0, i * (s // ts) + j)),
            pl.BlockSpec((ts // CHUNK, mw, CHUNK), lambda i, j: (i * (s // ts) + j, 0, 0)),
            pl.BlockSpec((SUBLANES, CHUNK), const2),
            pl.BlockSpec((1, mw), const2),
        ],
        out_specs=pl.BlockSpec((None, ts, mw), lambda i, j: (i, j, 0)),
        scratch_shapes=[
            pltpu.VMEM((ts // CHUNK, SUBLANES, CHUNK), F32),
            pltpu.VMEM((ts // CHUNK, SUBLANES, CHUNK), F32),
            pltpu.VMEM((ts // CHUNK, CHUNK, LANES), F32),
            pltpu.VMEM((ts // CHUNK, CHUNK, LANES), F32),
            pltpu.VMEM((n_heads, 2 * hd, hd), F32),
            pltpu.VMEM((SUBLANES, CHUNK), F32),
            pltpu.VMEM((1, LANES), F32),
        ],
        compiler_params=pltpu.CompilerParams(
            dimension_semantics=("arbitrary", "arbitrary"), vmem_limit_bytes=min(vmem, VMEM_BYTES - (8 << 20))),
        name="mixer",
    )(qkact, vo, vo, gates, vt, gbias, mhw)


def _out_ffn_kernel(ym_ref, u_ref, x_ref, wo_ref, gpost_ref, gpre_ref, wg_ref, wu_ref, wd_ref, gfpost_ref,
                    dww_ref, dwb_ref, cnw_ref, cnb_ref,
                    out_ref, a_ref, uext_ref, yst_ref, yc_ref,
                    *, ff_chunk, dw_taps, tiles_per_seq, row_blk):
    i = pl.program_id(0)
    tm = x_ref.shape[0]
    mw = ym_ref.shape[1]
    cw = u_ref.shape[1]
    P, n_blk = CONV_ROW_STRIDE, CONV_BLOCK_ROWS
    u_pad = uext_ref.shape[2] - tm
    n_u_slabs = cw // LANES
    slot = i % 2
    prev = 1 - slot

    @pl.when(i == 0)
    def _():
        uext_ref[1] = jnp.zeros(uext_ref.shape[1:], F32)
        yc_ref[1] = jnp.zeros(yc_ref.shape[1:], BF16)

    seq_start = (jnp.minimum(i, pl.num_programs(0) - 2) % tiles_per_seq) == 0
    for g in range(n_u_slabs):
        uext_ref[slot, g, 0:u_pad, :] = jnp.where(seq_start, 0.0, uext_ref[prev, g, tm:tm + u_pad, :])
    for blk in range(tm // row_blk):
        rows = slice(blk * row_blk, (blk + 1) * row_blk)
        for g in range(n_u_slabs):
            uext_ref[slot, g, u_pad + rows.start:u_pad + rows.stop, :] = (
                u_ref[rows, g * LANES:(g + 1) * LANES].astype(F32))

    def dw_conv(g, ph, r0, after):
        acc = jnp.broadcast_to(dwb_ref[g] + after, (n_blk, LANES))
        for k in range(dw_taps):
            start = u_pad - (dw_taps - 1) + k + ph + r0
            acc = acc + uext_ref[slot, g, pl.ds(start, n_blk, stride=P), :] * dww_ref[g, k:k + 1, :]
        mu = jnp.mean(acc, axis=-1, keepdims=True)
        var = jnp.maximum(jnp.mean(acc * acc, axis=-1, keepdims=True) - mu * mu, 0.0)
        yn = (acc - mu) * lax.rsqrt(var + NORM_EPS) * cnw_ref[g] + cnb_ref[g]
        y = yn * _sigmoid(yn)
        yst_ref[g, pl.ds(ph + r0, n_blk, stride=P), :] = y
        return y

    conv_pieces = [(g, ph, r0) for g in range(n_u_slabs) for ph in range(P) for r0 in range(0, tm, n_blk * P)]

    mix = (jnp.dot(ym_ref[...], wo_ref[0:mw, :], preferred_element_type=F32)
           + jnp.dot(yc_ref[prev], wo_ref[mw:, :], preferred_element_type=F32))
    out_ref[...] = x_ref[...] + _rms(mix, gpost_ref[...])
    hn = _rms(out_ref[...], gpre_ref[...]).astype(BF16)
    d_ff = wg_ref.shape[1]
    n_chunks = d_ff // ff_chunk
    per_chunk = -(-len(conv_pieces) // n_chunks)
    tie = None
    for c in range(n_chunks):
        sl = slice(c * ff_chunk, (c + 1) * ff_chunk)
        g = jnp.dot(hn, wg_ref[:, sl], preferred_element_type=F32)
        u = jnp.dot(hn, wu_ref[:, sl], preferred_element_type=F32)
        act = g * _sigmoid(g) * u
        if tie is not None:
            act = act + tie
        a_ref[:, sl] = act.astype(BF16)
        after = jnp.where(u[tm - 1:tm, ff_chunk - LANES:] > jnp.inf, 1.0, 0.0)
        tie = None
        for piece in conv_pieces[c * per_chunk:(c + 1) * per_chunk]:
            z = jnp.where(dw_conv(*piece, after)[0:1, :] > jnp.inf, 1.0, 0.0)
            tie = z if tie is None else tie + z
        if tie is not None:
            tie = jnp.concatenate([tie] * (ff_chunk // LANES), axis=1)
    for g in range(n_u_slabs):
        yc_ref[slot, :, g * LANES:(g + 1) * LANES] = yst_ref[g].astype(BF16)
    ff = jnp.dot(a_ref[...], wd_ref[...], preferred_element_type=F32)
    out_ref[...] = out_ref[...] + _rms(ff, gfpost_ref[...])


def _out_ffn(ym2d, u2d, x2d, wo, gpost, gpre, wg, wu, wd, gfpost, dww, dwb, cnw, cnb, *, tm, tiles_per_seq):
    t, d = x2d.shape
    mw = ym2d.shape[1]
    n_u_slabs, dw_taps = dww.shape[0], dww.shape[1]
    cw = n_u_slabs * LANES
    d_mix = mw + cw
    d_ff = wg.shape[1]
    n = t // tm
    assert d_ff % MXU_N == 0 and u2d.shape[1] == cw
    u_pad = -(-(dw_taps - 1) // SUBLANES) * SUBLANES
    const = lambda i: (0, 0)
    const3 = lambda i: (0, 0, 0)
    ffn_tile = lambda i: (jnp.maximum(i - 1, 0), 0)
    conv_tile = lambda i: (jnp.minimum(i, n - 1), 0)
    resident = functools.partial(pl.BlockSpec, index_map=const, pipeline_mode=pl.Buffered(1))
    vmem = (2 * tm * mw * 2 + 2 * tm * cw * 2 + 4 * tm * d * 4 + (d_mix * d + 3 * d * d_ff) * 2 + tm * d_ff * 2
            + (2 * (tm + u_pad) + tm) * cw * 4 + 2 * tm * cw * 2 + 8 * tm * d * 4)
    kern = functools.partial(_out_ffn_kernel, ff_chunk=MXU_N, dw_taps=dw_taps, tiles_per_seq=tiles_per_seq,
                             row_blk=64)
    return pl.pallas_call(
        kern,
        out_shape=jax.ShapeDtypeStruct((t, d), F32),
        grid=(n + 1,),
        in_specs=[
            pl.BlockSpec((tm, mw), ffn_tile),
            pl.BlockSpec((tm, cw), conv_tile),
            pl.BlockSpec((tm, d), ffn_tile),
            resident((d_mix, d)),
            pl.BlockSpec((1, d), const),
            pl.BlockSpec((1, d), const),
            resident((d, d_ff)),
            resident((d, d_ff)),
            resident((d_ff, d)),
            pl.BlockSpec((1, d), const),
            pl.BlockSpec(dww.shape, const3),
            pl.BlockSpec(dwb.shape, const3),
            pl.BlockSpec(cnw.shape, const3),
            pl.BlockSpec(cnb.shape, const3),
        ],
        out_specs=pl.BlockSpec((tm, d), ffn_tile),
        scratch_shapes=[
            pltpu.VMEM((tm, d_ff), BF16),
            pltpu.VMEM((2, n_u_slabs, tm + u_pad, LANES), F32),
            pltpu.VMEM((n_u_slabs, tm, LANES), F32),
            pltpu.VMEM((2, tm, cw), BF16),
        ],
        compiler_params=pltpu.CompilerParams(
            dimension_semantics=("arbitrary",), vmem_limit_bytes=min(vmem, VMEM_BYTES - (8 << 20))),
        name="out_ffn",
    )(ym2d, u2d, x2d, wo, gpost, gpre, wg, wu, wd, gfpost, dww, dwb, cnw, cnb)


def _pick_tile(n, target):
    t = min(n, target)
    while n % t:
        t //= 2
    return t


def _slabs(a):
    a = a.astype(F32).reshape(a.shape[0] if a.ndim == 2 else 1, -1, LANES)
    return jnp.transpose(a, (1, 0, 2))


def _layer(h, ln_mix_pre, ln_mix_post, w_in, qk_conv_w, qk_conv_b, i_bias, f_bias, mh_norm_w,
           dw_conv_w, dw_conv_b, conv_norm_w, conv_norm_b, w_out, ln_ffn_pre, ln_ffn_post,
           w_gate, w_up, w_down):
    b, s, d = h.shape
    n_heads = i_bias.shape[0]
    mw = mh_norm_w.shape[0]
    n_gate = 2 * n_heads
    row = lambda a: a.reshape(1, -1).astype(F32)

    w_main = jnp.concatenate([w_in[:, :4 * mw], w_in[:, 4 * mw + n_gate:]], axis=1).astype(BF16)
    w_gates = jnp.pad(w_in[:, 4 * mw:4 * mw + n_gate], ((0, 0), (0, LANES - n_gate))).astype(BF16)
    gbias = jnp.broadcast_to(
        jnp.pad(jnp.concatenate([i_bias, f_bias]), (0, SUBLANES - n_gate)).reshape(SUBLANES, 1).astype(F32),
        (SUBLANES, CHUNK))

    x2d = h.reshape(b * s, d)
    ts = _pick_tile(s, 512)
    assert ts % CHUNK == 0
    qkact, vo, u, gates, vt = _in_proj(x2d, row(ln_mix_pre), w_main, w_gates, _slabs(qk_conv_w),
                                       _slabs(qk_conv_b), tm=ts, mw=mw, tiles_per_seq=s // ts)

    ym = _mixer(qkact.reshape(b, s, -1), vo.reshape(b, s, -1), gates, vt, gbias, row(mh_norm_w),
                n_heads=n_heads, ts=_pick_tile(s, 2 * ts))

    out = _out_ffn(ym.reshape(b * s, -1), u, x2d, w_out.astype(BF16), row(ln_mix_post), row(ln_ffn_pre),
                   w_gate.astype(BF16), w_up.astype(BF16), w_down.astype(BF16), row(ln_ffn_post),
                   _slabs(dw_conv_w), _slabs(dw_conv_b), _slabs(conv_norm_w), _slabs(conv_norm_b),
                   tm=ts, tiles_per_seq=s // ts)
    return out.reshape(b, s, d)


def kernel(x, ln_mix_pre, ln_mix_post, w_in, qk_conv_w, qk_conv_b, i_bias, f_bias, mh_norm_w,
           dw_conv_w, dw_conv_b, conv_norm_w, conv_norm_b, w_out, ln_ffn_pre, ln_ffn_post,
           w_gate, w_up, w_down):
    h = x
    for l in range(ln_mix_pre.shape[0]):
        h = _layer(h, ln_mix_pre[l], ln_mix_post[l], w_in[l], qk_conv_w[l], qk_conv_b[l], i_bias[l],
                   f_bias[l], mh_norm_w[l], dw_conv_w[l], dw_conv_b[l], conv_norm_w[l], conv_norm_b[l],
                   w_out[l], ln_ffn_pre[l], ln_ffn_post[l], w_gate[l], w_up[l], w_down[l])
    return h
```
